```python
import math
import jax, jax.numpy as jnp
from jax import lax
import numpy as np

D_MODEL = 1024
BATCH = 16
SEQ = 256
DEPTH = 4
DEC_BATCH = 8
DEC_SEQ = 4096
PAST_LEN = 256

GRID_W = 64
MIX_W = D_MODEL
N_MIXERS = 4
GROUP_W = MIX_W // N_MIXERS
SSM_CH = 16
SSM_GROUPS = GROUP_W // SSM_CH
SSM_STATE = 64
FNET_HEADS = 4
FNET_CH = GROUP_W // FNET_HEADS
POOL_WINDOWS = (2, 4, 8, 16)
POOL_CH = GROUP_W // len(POOL_WINDOWS)
SGU_HEADS = 4
SGU_CH = GROUP_W // SGU_HEADS
SGU_CHUNK = 128
D_FF = 2816
IN_COLS = 5 * GROUP_W
N_MOD = 9
ALPHA = (2 * DEPTH) ** 0.25
BETA = (8 * DEPTH) ** -0.25
LN_EPS = 1e-5

kernel_name = "hybrid_s5_fnet_pool_sgu_diffusion_step"


def layer_norm(x, g, b):
    x32 = x.astype(jnp.float32)
    mu = jnp.mean(x32, -1, keepdims=True)
    var = jnp.mean(jnp.square(x32 - mu), -1, keepdims=True)
    y = (x32 - mu) * lax.rsqrt(var + LN_EPS) * g.astype(jnp.float32) + b.astype(jnp.float32)
    return y.astype(x.dtype)


def swiglu(h, w_in, w_out):
    a, g = jnp.split(h @ w_in, 2, axis=-1)
    return (jax.nn.silu(g) * a) @ w_out


def sincos_2d(n_tokens):
    rows = n_tokens // GRID_W
    quarter = D_MODEL // 4
    omega = 1.0 / (10000.0 ** (jnp.arange(quarter, dtype=jnp.float32) / quarter))
    ang_r = jnp.arange(rows, dtype=jnp.float32)[:, None] * omega
    ang_c = jnp.arange(GRID_W, dtype=jnp.float32)[:, None] * omega
    emb_r = jnp.concatenate([jnp.sin(ang_r), jnp.cos(ang_r)], -1)
    emb_c = jnp.concatenate([jnp.sin(ang_c), jnp.cos(ang_c)], -1)
    half = D_MODEL // 2
    pos = jnp.concatenate([jnp.broadcast_to(emb_r[:, None], (rows, GRID_W, half)),
                           jnp.broadcast_to(emb_c[None], (rows, GRID_W, half))], -1)
    return pos.reshape(rows * GRID_W, D_MODEL)


def _ssm_combine(e1, e2):
    a1, b1 = e1
    a2, b2 = e2
    return a1 * a2, a2 * b1 + b2


def s5_mixer(u, lam_re, lam_im, log_dt, b_re, b_im, c_re, c_im, d_skip, glu_w, glu_b, h0_re, h0_im):
    f32 = jnp.float32
    nb, n, _ = u.shape
    u32 = u.astype(f32)
    ug = u32.reshape(nb, n, SSM_GROUPS, SSM_CH).astype(jnp.complex64)
    y = u32 * d_skip.astype(f32)
    fin_re, fin_im = [], []
    for dirn in range(2):
        lam = lax.complex(lam_re[dirn].astype(f32), lam_im[dirn].astype(f32))
        dt = jnp.exp(log_dt[dirn].astype(f32))[:, None]
        a_bar = jnp.exp(lam * dt)
        b_mat = lax.complex(b_re[dirn].astype(f32), b_im[dirn].astype(f32))
        b_bar = ((a_bar - 1.0) / lam)[..., None] * b_mat
        bu = jnp.einsum("blgh,gph->blgp", ug, b_bar)
        h0 = lax.complex(h0_re[:, dirn].astype(f32), h0_im[:, dirn].astype(f32))
        rev = dirn == 1
        first = n - 1 if rev else 0
        bu = bu.at[:, first].add(a_bar * h0)
        _, hs = lax.associative_scan(_ssm_combine, (jnp.broadcast_to(a_bar, bu.shape), bu),
                                     axis=1, reverse=rev)
        c_mat = lax.complex(c_re[dirn].astype(f32), c_im[dirn].astype(f32))
        y = y + jnp.einsum("blgp,ghp->blgh", hs, c_mat).real.reshape(nb, n, GROUP_W)
        fin = hs[:, 0] if rev else hs[:, -1]
        fin_re.append(jnp.real(fin))
        fin_im.append(jnp.imag(fin))
    y = jax.nn.gelu(y)
    y = y * jax.nn.sigmoid(y @ glu_w.astype(f32) + glu_b.astype(f32))
    return y, jnp.stack(fin_re, 1), jnp.stack(fin_im, 1)


def fnet_mixer(z, fnet_w):
    nb, n, _ = z.shape
    zh = z.astype(jnp.float32).reshape(nb, n, FNET_HEADS, FNET_CH)
    f = jnp.fft.fft2(zh, axes=(1, 3), norm="ortho").real
    return jnp.einsum("blgc,gcd->blgd", f, fnet_w.astype(jnp.float32)).reshape(nb, n, GROUP_W)


def pool_mixer(z, pool_w, pool_scale):
    f32 = jnp.float32
    nb, n, _ = z.shape
    ng = len(POOL_WINDOWS)
    z32 = z.astype(f32).reshape(nb, n, ng, POOL_CH)
    cs = jnp.concatenate([jnp.zeros((nb, 1, ng, POOL_CH), f32), jnp.cumsum(z32, axis=1)], axis=1)
    t = jnp.arange(n)
    pooled = []
    for g, w in enumerate(POOL_WINDOWS):
        lo = jnp.clip(t - w // 2, 0, n)
        hi = jnp.clip(t + w - w // 2, 0, n)
        win_sum = jnp.take(cs[:, :, g], hi, axis=1) - jnp.take(cs[:, :, g], lo, axis=1)
        pooled.append(win_sum / (hi - lo).astype(f32)[None, :, None])
    p = jnp.stack(pooled, 2) - z32
    y = jnp.einsum("blgc,gcd->blgd", p, pool_w.astype(f32)).reshape(nb, n, GROUP_W)
    return y * pool_scale.astype(f32)


def sgu_mixer(z, sgu_w, sgu_b):
    f32 = jnp.float32
    nb, n, _ = z.shape
    z = jax.nn.gelu(z.astype(f32))
    u, v = z[..., :GROUP_W], z[..., GROUP_W:]
    v = v.reshape(nb, n // SGU_CHUNK, SGU_CHUNK, SGU_HEADS, SGU_CH)
    mu = jnp.mean(v, -1, keepdims=True)
    var = jnp.mean(jnp.square(v - mu), -1, keepdims=True)
    v = (v - mu) * lax.rsqrt(var + LN_EPS)
    s = jnp.einsum("bnsgc,gts->bntgc", v, sgu_w.astype(f32)) + sgu_b.astype(f32).T[None, None, :, :, None]
    return u * s.reshape(nb, n, GROUP_W)


def token_mixers(h, lp, h0_re, h0_im):
    nb, n, _ = h.shape
    z = h @ lp["w_mix_in"]
    z_a = z[..., :GROUP_W]
    z_b = z[..., GROUP_W:2 * GROUP_W]
    z_c = z[..., 2 * GROUP_W:3 * GROUP_W]
    z_d = z[..., 3 * GROUP_W:]
    y_a, fin_re, fin_im = s5_mixer(z_a, lp["ssm_lam_re"], lp["ssm_lam_im"], lp["ssm_log_dt"],
                                   lp["ssm_b_re"], lp["ssm_b_im"], lp["ssm_c_re"], lp["ssm_c_im"],
                                   lp["ssm_d"], lp["ssm_glu_w"], lp["ssm_glu_b"], h0_re, h0_im)
    y_b = fnet_mixer(z_b, lp["fnet_w"])
    y_c = pool_mixer(z_c, lp["pool_w"], lp["pool_scale"])
    y_d = sgu_mixer(z_d, lp["sgu_w"], lp["sgu_b"])
    y = jnp.stack([y_a, y_b, y_c, y_d], axis=2)
    y = y * lax.rsqrt(jnp.mean(jnp.square(y), -1, keepdims=True) + LN_EPS)
    y = y.reshape(nb, n, MIX_W) * lp["mix_norm_g"].astype(jnp.float32)
    return y.astype(h.dtype) @ lp["w_mix_out"], fin_re, fin_im


def modulation(cond, w_ada, b_ada):
    return (jax.nn.silu(cond) @ w_ada + b_ada).reshape(cond.shape[0], N_MOD, D_MODEL)


def trunk_layer(x, mod, lp, h0_re, h0_im):
    m = [mod[:, k][:, None, :] for k in range(N_MOD)]
    h = x * (1.0 + m[1]) + m[0]
    x = layer_norm(ALPHA * x + 0.5 * m[2] * swiglu(h, lp["ffn_w_in"][0], lp["ffn_w_out"][0]),
                   lp["ln_g"][0], lp["ln_b"][0])
    h = x * (1.0 + m[4]) + m[3]
    y, fin_re, fin_im = token_mixers(h, lp, h0_re, h0_im)
    x = layer_norm(ALPHA * x + m[5] * y, lp["ln_g"][1], lp["ln_b"][1])
    h = x * (1.0 + m[7]) + m[6]
    x = layer_norm(ALPHA * x + 0.5 * m[8] * swiglu(h, lp["ffn_w_in"][1], lp["ffn_w_out"][1]),
                   lp["ln_g"][2], lp["ln_b"][2])
    return x, fin_re, fin_im


def setup_inputs(seed: int = 0) -> dict:
    key = jax.random.key(seed)
    ks = jax.random.split(key, 32)
    f32 = jnp.float32
    nrm = lambda k, shape, s: jax.random.normal(k, shape, f32) * s
    st_shape = (DEC_BATCH, DEPTH, 2, SSM_GROUPS, SSM_STATE)
    lam_im = jnp.broadcast_to(math.pi * jnp.arange(SSM_STATE, dtype=f32), (DEPTH, 2, SSM_GROUPS, SSM_STATE))
    return {
        "x_prompt": nrm(ks[0], (BATCH, SEQ, D_MODEL), 1.0),
        "x_sample": nrm(ks[1], (DEC_BATCH, DEC_SEQ, D_MODEL), 1.0),
        "c": nrm(ks[2], (DEC_BATCH, D_MODEL), 1.0),
        "state_s5_re": nrm(ks[3], st_shape, 0.1),
        "state_s5_im": nrm(ks[4], st_shape, 0.1),
        "c_ctx": nrm(ks[5], (D_MODEL,), 1.0),
        "w_ada": nrm(ks[6], (DEPTH, D_MODEL, N_MOD * D_MODEL), 0.5 * D_MODEL ** -0.5),
        "b_ada": nrm(ks[7], (DEPTH, N_MOD * D_MODEL), 0.01),
        "ffn_w_in": nrm(ks[8], (DEPTH, 2, D_MODEL, 2 * D_FF), D_MODEL ** -0.5),
        "ffn_w_out": nrm(ks[9], (DEPTH, 2, D_FF, D_MODEL), BETA * D_FF ** -0.5),
        "w_mix_in": nrm(ks[10], (DEPTH, D_MODEL, IN_COLS), D_MODEL ** -0.5),
        "w_mix_out": nrm(ks[11], (DEPTH, MIX_W, D_MODEL), BETA * MIX_W ** -0.5),
        "mix_norm_g": 1.0 + nrm(ks[12], (DEPTH, MIX_W), 0.01),
        "ssm_lam_re": -0.5 + nrm(ks[13], (DEPTH, 2, SSM_GROUPS, SSM_STATE), 0.01),
        "ssm_lam_im": lam_im + nrm(ks[14], (DEPTH, 2, SSM_GROUPS, SSM_STATE), 0.01),
        "ssm_log_dt": jax.random.uniform(ks[15], (DEPTH, 2, SSM_GROUPS), f32, math.log(1e-3), math.log(1e-1)),
        "ssm_b_re": nrm(ks[16], (DEPTH, 2, SSM_GROUPS, SSM_STATE, SSM_CH), (2 * SSM_CH) ** -0.5),
        "ssm_b_im": nrm(ks[17], (DEPTH, 2, SSM_GROUPS, SSM_STATE, SSM_CH), (2 * SSM_CH) ** -0.5),
        "ssm_c_re": nrm(ks[18], (DEPTH, 2, SSM_GROUPS, SSM_CH, SSM_STATE), (2 * SSM_STATE) ** -0.5),
        "ssm_c_im": nrm(ks[19], (DEPTH, 2, SSM_GROUPS, SSM_CH, SSM_STATE), (2 * SSM_STATE) ** -0.5),
        "ssm_d": nrm(ks[20], (DEPTH, GROUP_W), 1.0),
        "ssm_glu_w": nrm(ks[21], (DEPTH, GROUP_W, GROUP_W), GROUP_W ** -0.5),
        "ssm_glu_b": nrm(ks[22], (DEPTH, GROUP_W), 0.01),
        "fnet_w": nrm(ks[23], (DEPTH, FNET_HEADS, FNET_CH, FNET_CH), FNET_CH ** -0.5),
        "pool_w": nrm(ks[24], (DEPTH, len(POOL_WINDOWS), POOL_CH, POOL_CH), POOL_CH ** -0.5),
        "pool_scale": 1.0 + nrm(ks[25], (DEPTH, GROUP_W), 0.1),
        "sgu_w": nrm(ks[26], (DEPTH, SGU_HEADS, SGU_CHUNK, SGU_CHUNK), SGU_CHUNK ** -0.5),
        "sgu_b": 1.0 + nrm(ks[27], (DEPTH, SGU_HEADS, SGU_CHUNK), 0.01),
        "ln_g": 1.0 + nrm(ks[28], (DEPTH, 3, D_MODEL), 0.01),
        "ln_b": nrm(ks[29], (DEPTH, 3, D_MODEL), 0.01),
    }


def reference(x_prompt, x_sample, c, state_s5_re, state_s5_im, c_ctx, w_ada, b_ada, ffn_w_in, ffn_w_out,
              w_mix_in, w_mix_out, mix_norm_g, ssm_lam_re, ssm_lam_im, ssm_log_dt, ssm_b_re, ssm_b_im,
              ssm_c_re, ssm_c_im, ssm_d, ssm_glu_w, ssm_glu_b, fnet_w, pool_w, pool_scale, sgu_w, sgu_b,
              ln_g, ln_b):
    n_ctx_batch = x_prompt.shape[0]
    cond_ctx = jnp.broadcast_to(c_ctx, (n_ctx_batch, D_MODEL))
    zero_state = jnp.zeros((n_ctx_batch, 2, SSM_GROUPS, SSM_STATE), jnp.float32)
    xp = x_prompt
    xs = x_sample + sincos_2d(x_sample.shape[1]).astype(x_sample.dtype)[None]
    new_re, new_im = [], []
    for i in range(DEPTH):
        lp = {
            "ffn_w_in": ffn_w_in[i], "ffn_w_out": ffn_w_out[i],
            "w_mix_in": w_mix_in[i], "w_mix_out": w_mix_out[i], "mix_norm_g": mix_norm_g[i],
            "ssm_lam_re": ssm_lam_re[i], "ssm_lam_im": ssm_lam_im[i], "ssm_log_dt": ssm_log_dt[i],
            "ssm_b_re": ssm_b_re[i], "ssm_b_im": ssm_b_im[i], "ssm_c_re": ssm_c_re[i], "ssm_c_im": ssm_c_im[i],
            "ssm_d": ssm_d[i], "ssm_glu_w": ssm_glu_w[i], "ssm_glu_b": ssm_glu_b[i],
            "fnet_w": fnet_w[i], "pool_w": pool_w[i], "pool_scale": pool_scale[i],
            "sgu_w": sgu_w[i], "sgu_b": sgu_b[i], "ln_g": ln_g[i], "ln_b": ln_b[i],
        }
        xp, fin_re, fin_im = trunk_layer(xp, modulation(cond_ctx, w_ada[i], b_ada[i]), lp,
                                         zero_state, zero_state)
        new_re.append(fin_re)
        new_im.append(fin_im)
        xs, _, _ = trunk_layer(xs, modulation(c, w_ada[i], b_ada[i]), lp,
                               state_s5_re[:, i], state_s5_im[:, i])
    new_s5_re = jnp.stack(new_re, axis=1)
    new_s5_im = jnp.stack(new_im, axis=1)
    return (xp, xs, new_s5_re, new_s5_im)
```

```python
import functools
import math

import numpy as np
import jax
import jax.numpy as jnp
from jax import lax
from jax.experimental import pallas as pl
from jax.experimental.pallas import tpu as pltpu

F32 = jnp.float32
BF16 = jnp.bfloat16

D_MODEL = 1024
DEPTH = 4
GRID_W = 64
GROUP_W = 256
N_MIXERS = 4
SSM_CH = 16
SSM_GROUPS = 16
SSM_STATE = 64
N_STATE = SSM_GROUPS * SSM_STATE
FNET_HEADS = 4
FNET_CH = 64
POOL_CH = 64
SGU_HEADS = 4
SGU_CH = 64
SGU_CHUNK = 128
D_FF = 2816
IN_COLS = 5 * GROUP_W
N_MOD = 9
ALPHA = (2 * DEPTH) ** 0.25
LN_EPS = 1e-5

FF_CHUNK = 1408
POOL_HALO = 8
DFT_RADIX = 64
VMEM_LIMIT = 56 * 1024 * 1024


def _dot(a, b):
    return jnp.dot(a, b, preferred_element_type=F32)


def _sigmoid(x):
    return 1.0 / (1.0 + jnp.exp(-x))


def _gelu(x):
    return 0.5 * x * (1.0 + jnp.tanh(math.sqrt(2.0 / math.pi) * (x + 0.044715 * (x * x * x))))


def _layer_norm(r, g, b):
    mu = jnp.mean(r, axis=-1, keepdims=True)
    d = r - mu
    var = jnp.mean(d * d, axis=-1, keepdims=True)
    return d * lax.rsqrt(var + LN_EPS) * g + b


def _swiglu(h, w_in_ref, w_out_ref):
    acc = None
    for c in range(D_FF // FF_CHUNK):
        lo = c * FF_CHUNK
        a = _dot(h, w_in_ref[:, lo:lo + FF_CHUNK])
        g = _dot(h, w_in_ref[:, D_FF + lo:D_FF + lo + FF_CHUNK])
        act = (g * _sigmoid(g) * a).astype(BF16)
        o = _dot(act, w_out_ref[lo:lo + FF_CHUNK, :])
        acc = o if acc is None else acc + o
    return acc


def _ffn_sublayer(x, shift, scale, gate, w_in_ref, w_out_ref, g, b):
    h = (x * (1.0 + scale) + shift).astype(BF16)
    f = _swiglu(h, w_in_ref, w_out_ref)
    return _layer_norm(ALPHA * x + 0.5 * gate * f, g, b)


def _mod_kernel(c_ref, w_ref, b_ref, o_ref):
    c = c_ref[...]
    s = c * _sigmoid(c)
    o_ref[0] = jnp.dot(s, w_ref[0], preferred_element_type=F32,
                       precision=lax.Precision.HIGHEST) + b_ref[0]


def _modulation(cond, w_ada, b_ada):
    rows = cond.shape[0]
    tn = 1536
    n_col = N_MOD * D_MODEL
    return pl.pallas_call(
        _mod_kernel,
        grid=(DEPTH, n_col // tn),
        in_specs=[
            pl.BlockSpec((rows, D_MODEL), lambda l, j: (0, 0)),
            pl.BlockSpec((1, D_MODEL, tn), lambda l, j: (l, 0, j)),
            pl.BlockSpec((1, 1, tn), lambda l, j: (l, 0, j)),
        ],
        out_specs=pl.BlockSpec((1, rows, tn), lambda l, j: (l, 0, j)),
        out_shape=jax.ShapeDtypeStruct((DEPTH, rows, n_col), F32),
        compiler_params=pltpu.CompilerParams(
            dimension_semantics=("parallel", "parallel"), vmem_limit_bytes=VMEM_LIMIT),
        name="adaln_modulation",
    )(cond, w_ada, b_ada.reshape(DEPTH, 1, n_col))


def _dft_table_kernel(ca_ref, sa_ref, cb_ref, sb_ref, c_ref, s_ref):
    ca = ca_ref[0]
    sa = sa_ref[0]
    cb = cb_ref[...]
    sb = sb_ref[...]
    c_ref[...] = (ca * cb - sa * sb).astype(BF16)
    s_ref[...] = (sa * cb + ca * sb).astype(BF16)


def _dft_tables(n):
    r = DFT_RADIX
    k = np.arange(n, dtype=np.int64)
    ang_a = 2.0 * np.pi * ((r * np.arange(n // r, dtype=np.int64)[:, None] * k[None]) % n) / n
    ang_b = 2.0 * np.pi * ((np.arange(r, dtype=np.int64)[:, None] * k[None]) % n) / n
    ca = jnp.asarray(np.cos(ang_a), F32).reshape(n // r, 1, n)
    sa = jnp.asarray(np.sin(ang_a), F32).reshape(n // r, 1, n)
    cb = jnp.asarray(np.cos(ang_b), F32)
    sb = jnp.asarray(np.sin(ang_b), F32)
    return pl.pallas_call(
        _dft_table_kernel,
        grid=(n // r,),
        in_specs=[
            pl.BlockSpec((1, 1, n), lambda i: (i, 0, 0)),
            pl.BlockSpec((1, 1, n), lambda i: (i, 0, 0)),
            pl.BlockSpec((r, n), lambda i: (0, 0)),
            pl.BlockSpec((r, n), lambda i: (0, 0)),
        ],
        out_specs=[pl.BlockSpec((r, n), lambda i: (i, 0)), pl.BlockSpec((r, n), lambda i: (i, 0))],
        out_shape=[jax.ShapeDtypeStruct((n, n), BF16), jax.ShapeDtypeStruct((n, n), BF16)],
        compiler_params=pltpu.CompilerParams(
            dimension_semantics=("parallel",), vmem_limit_bytes=VMEM_LIMIT),
        name="dft_tables",
    )(ca, sa, cb, sb)


def _ffn_mixin_kernel(has_pos, *refs):
    if has_pos:
        x_ref, pos_ref = refs[0], refs[1]
        refs = refs[2:]
        x = x_ref[0] + pos_ref[...]
    else:
        x_ref = refs[0]
        refs = refs[1:]
        x = x_ref[0]
    (mod_ref, w_in_ref, w_out_ref, lng_ref, lnb_ref, wmix_ref, fcs_ref,
     x1_ref, za_ref, gre_ref, gim_ref, zc_ref, zu_ref, zv_ref) = refs
    m = lambda k: mod_ref[0, k:k + 1, :]
    x1 = _ffn_sublayer(x, m(0), m(1), m(2), w_in_ref, w_out_ref, lng_ref[0:1, :], lnb_ref[0:1, :])
    x1_ref[0] = x1
    hm = (x1 * (1.0 + m(4)) + m(3)).astype(BF16)
    z = _dot(hm, wmix_ref[...])
    za_ref[...] = z[:, 0:GROUP_W]
    g = _dot(z[:, GROUP_W:2 * GROUP_W].astype(BF16), fcs_ref[...])
    gre_ref[...] = g[:, 0:GROUP_W].astype(BF16)
    gim_ref[...] = g[:, GROUP_W:2 * GROUP_W].astype(BF16)
    zc_ref[...] = z[:, 2 * GROUP_W:3 * GROUP_W]
    zu_ref[...] = z[:, 3 * GROUP_W:4 * GROUP_W]
    zv_ref[...] = z[:, 4 * GROUP_W:5 * GROUP_W]


def _const_spec(shape):
    nd = len(shape)
    return pl.BlockSpec(shape, lambda b, i: (0,) * nd, pipeline_mode=pl.Buffered(1))


def _ffn_mixin(x, pos, mod, w_in, w_out, ln_g, ln_b, w_mix, fcs, tt):
    nb, n, _ = x.shape
    has_pos = pos is not None
    tm_spec = pl.BlockSpec((tt, GROUP_W), lambda b, i: (i, b))
    in_specs = [pl.BlockSpec((1, tt, D_MODEL), lambda b, i: (b, i, 0))]
    args = [x]
    if has_pos:
        in_specs.append(pl.BlockSpec((tt, D_MODEL), lambda b, i: (i, 0)))
        args.append(pos)
    in_specs += [
        pl.BlockSpec((1, N_MOD, D_MODEL), lambda b, i: (b, 0, 0)),
        _const_spec(w_in.shape), _const_spec(w_out.shape),
        _const_spec(ln_g.shape), _const_spec(ln_b.shape),
        _const_spec(w_mix.shape), _const_spec(fcs.shape),
    ]
    args += [mod, w_in, w_out, ln_g, ln_b, w_mix, fcs]
    tm_shape = lambda dt: jax.ShapeDtypeStruct((n, nb * GROUP_W), dt)
    return pl.pallas_call(
        functools.partial(_ffn_mixin_kernel, has_pos),
        grid=(nb, n // tt),
        in_specs=in_specs,
        out_specs=[pl.BlockSpec((1, tt, D_MODEL), lambda b, i: (b, i, 0))] + [tm_spec] * 6,
        out_shape=[jax.ShapeDtypeStruct(x.shape, F32), tm_shape(F32), tm_shape(BF16), tm_shape(BF16),
                   tm_shape(F32), tm_shape(F32), tm_shape(F32)],
        compiler_params=pltpu.CompilerParams(
            dimension_semantics=("parallel", "parallel"), vmem_limit_bytes=VMEM_LIMIT),
        name="ffn_mixin",
    )(*args)


def _s5_scan_kernel(nb, tt, zf_ref, zb_ref, bmat_ref, cmat_ref, a_ref, h0_ref,
                    yf_ref, yb_ref, fin_ref, hsf_ref, hsb_ref, st_ref):
    i = pl.program_id(0)

    @pl.when(i == 0)
    def _():
        st_ref[...] = h0_ref[...]

    hsf_ref[...] = _dot(zf_ref[...].astype(BF16), bmat_ref[0])
    hsb_ref[...] = _dot(zb_ref[...].astype(BF16), bmat_ref[1])

    cw = 4096 // nb
    for c in range(N_STATE // cw):
        re = slice(c * cw, (c + 1) * cw)
        im = slice(N_STATE + c * cw, N_STATE + (c + 1) * cw)
        afr = jnp.broadcast_to(a_ref[0:1, re], (nb, cw))
        afi = jnp.broadcast_to(a_ref[1:2, re], (nb, cw))
        abr = jnp.broadcast_to(a_ref[2:3, re], (nb, cw))
        abi = jnp.broadcast_to(a_ref[3:4, re], (nb, cw))

        def step(t, carry, re=re, im=im, afr=afr, afi=afi, abr=abr, abi=abi):
            hfr, hfi, hbr, hbi = carry
            rf = pl.multiple_of(t * nb, nb)
            rb = pl.multiple_of((tt - 1 - t) * nb, nb)
            nfr = afr * hfr - afi * hfi + hsf_ref[pl.ds(rf, nb), re]
            nfi = afr * hfi + afi * hfr + hsf_ref[pl.ds(rf, nb), im]
            nbr = abr * hbr - abi * hbi + hsb_ref[pl.ds(rb, nb), re]
            nbi = abr * hbi + abi * hbr + hsb_ref[pl.ds(rb, nb), im]
            hsf_ref[pl.ds(rf, nb), re] = nfr
            hsf_ref[pl.ds(rf, nb), im] = nfi
            hsb_ref[pl.ds(rb, nb), re] = nbr
            hsb_ref[pl.ds(rb, nb), im] = nbi
            return nfr, nfi, nbr, nbi

        init = (st_ref[0, :, re], st_ref[0, :, im], st_ref[1, :, re], st_ref[1, :, im])
        hfr, hfi, hbr, hbi = lax.fori_loop(0, tt, step, init, unroll=4)
        st_ref[0, :, re] = hfr
        st_ref[0, :, im] = hfi
        st_ref[1, :, re] = hbr
        st_ref[1, :, im] = hbi

    yf_ref[...] = _dot(hsf_ref[...].astype(BF16), cmat_ref[0])
    yb_ref[...] = _dot(hsb_ref[...].astype(BF16), cmat_ref[1])
    fin_ref[...] = st_ref[...]


def _s5_scan(za, bmat, cmat, a_vec, h0, nb, n, tt):
    rows = tt * nb
    nt = n // tt
    cs = lambda shape: pl.BlockSpec(shape, lambda i: (0,) * len(shape))
    return pl.pallas_call(
        functools.partial(_s5_scan_kernel, nb, tt),
        grid=(nt,),
        in_specs=[
            pl.BlockSpec((rows, GROUP_W), lambda i: (i, 0)),
            pl.BlockSpec((rows, GROUP_W), lambda i: (nt - 1 - i, 0)),
            cs(bmat.shape), cs(cmat.shape), cs(a_vec.shape), cs(h0.shape),
        ],
        out_specs=[
            pl.BlockSpec((rows, GROUP_W), lambda i: (i, 0)),
            pl.BlockSpec((rows, GROUP_W), lambda i: (nt - 1 - i, 0)),
            cs(h0.shape),
        ],
        out_shape=[jax.ShapeDtypeStruct(za.shape, F32), jax.ShapeDtypeStruct(za.shape, F32),
                   jax.ShapeDtypeStruct(h0.shape, F32)],
        scratch_shapes=[pltpu.VMEM((rows, 2 * N_STATE), F32), pltpu.VMEM((rows, 2 * N_STATE), F32),
                        pltpu.VMEM(h0.shape, F32)],
        compiler_params=pltpu.CompilerParams(
            dimension_semantics=("arbitrary",), vmem_limit_bytes=VMEM_LIMIT),
        name="s5_scan",
    )(za, za, bmat, cmat, a_vec, h0)


def _seq_dft_kernel(c_ref, s_ref, gre_ref, gim_ref, o_ref):
    part = _dot(c_ref[...], gre_ref[...]) + _dot(s_ref[...], gim_ref[...])

    @pl.when(pl.program_id(1) == 0)
    def _():
        o_ref[...] = part

    @pl.when(pl.program_id(1) > 0)
    def _():
        o_ref[...] += part


def _seq_dft(ctab, stab, gre, gim):
    n, cols = gre.shape
    tl = min(n, 1024)
    tk = min(n, 512)
    return pl.pallas_call(
        _seq_dft_kernel,
        grid=(n // tl, n // tk),
        in_specs=[
            pl.BlockSpec((tl, tk), lambda i, k: (i, k)),
            pl.BlockSpec((tl, tk), lambda i, k: (i, k)),
            pl.BlockSpec((tk, cols), lambda i, k: (k, 0)),
            pl.BlockSpec((tk, cols), lambda i, k: (k, 0)),
        ],
        out_specs=pl.BlockSpec((tl, cols), lambda i, k: (i, 0)),
        out_shape=jax.ShapeDtypeStruct((n, cols), F32),
        compiler_params=pltpu.CompilerParams(
            dimension_semantics=("parallel", "arbitrary"), vmem_limit_bytes=VMEM_LIMIT),
        name="seq_dft",
    )(ctab, stab, gre, gim)


def _seg_mean(x, ones_ref):
    hi = x.astype(BF16)
    lo = (x - hi.astype(F32)).astype(BF16)
    return _dot(hi, ones_ref[...]) + _dot(lo, ones_ref[...])


def _group_norm(y, g):
    return y * lax.rsqrt(jnp.mean(y * y, axis=-1, keepdims=True) + LN_EPS) * g


def _mix_ffn_kernel(n, tt, dft_scale,
                    x_ref, mod_ref, za_ref, yf_ref, yb_ref, fr_ref, zc_ref, zcp_ref, zcn_ref, zu_ref, zv_ref,
                    dskip_ref, gluw_ref, glub_ref, fnw_ref, plw_ref, pls_ref, ones_ref, sguw_ref, sgub_ref,
                    mng_ref, wout_ref, w_in_ref, w_out_ref, lng_ref, lnb_ref, o_ref):
    i = pl.program_id(1)
    last = pl.num_programs(1) - 1
    m = lambda k: mod_ref[0, k:k + 1, :]
    lane_grp = lax.broadcasted_iota(jnp.int32, (1, GROUP_W), 1) // POOL_CH

    ya = za_ref[...] * dskip_ref[...] + yf_ref[...] + yb_ref[...]
    ya = _gelu(ya)
    ya = ya * _sigmoid(_dot(ya.astype(BF16), gluw_ref[...]) + glub_ref[...])

    yb = _dot((fr_ref[...] * dft_scale).astype(BF16), fnw_ref[...])

    zc = zc_ref[...]
    prev = jnp.where(i > 0, zcp_ref[...], 0.0)
    nxt = jnp.where(i < last, zcn_ref[...], 0.0)
    padded = jnp.concatenate([prev, zc, nxt], axis=0)
    rows = tt + 2 * POOL_HALO
    ahead = lambda v, k: pltpu.roll(v, rows - k, axis=0)
    s2 = padded + ahead(padded, 1)
    s4 = s2 + ahead(s2, 2)
    s8 = s4 + ahead(s4, 4)
    s16 = s8 + ahead(s8, 8)
    win = jnp.where(lane_grp == 0, ahead(s2, 7)[:tt],
                    jnp.where(lane_grp == 1, ahead(s4, 6)[:tt],
                              jnp.where(lane_grp == 2, ahead(s8, 4)[:tt], s16[:tt])))
    t_glob = i * tt + lax.broadcasted_iota(jnp.int32, (tt, GROUP_W), 0)
    half = jnp.left_shift(1, lane_grp)
    cnt = jnp.minimum(t_glob + half, n) - jnp.maximum(t_glob - half, 0)
    pooled = win / cnt.astype(F32)
    yc = _dot((pooled - zc).astype(BF16), plw_ref[...]) * pls_ref[...]

    u = _gelu(zu_ref[...])
    v = _gelu(zv_ref[...])
    dv = v - _seg_mean(v, ones_ref)
    vn = dv * lax.rsqrt(_seg_mean(dv * dv, ones_ref) + LN_EPS)
    parts = []
    for c in range(tt // SGU_CHUNK):
        vc = vn[c * SGU_CHUNK:(c + 1) * SGU_CHUNK]
        stacked = jnp.concatenate(
            [jnp.where(lane_grp == g, vc, 0.0).astype(BF16) for g in range(SGU_HEADS)], axis=0)
        parts.append(_dot(sguw_ref[...], stacked) + sgub_ref[...])
    yd = u * jnp.concatenate(parts, axis=0)

    acc = None
    for k, y in enumerate((ya, yb, yc, yd)):
        lo = k * GROUP_W
        yn = _group_norm(y, mng_ref[:, lo:lo + GROUP_W]).astype(BF16)
        o = _dot(yn, wout_ref[lo:lo + GROUP_W, :])
        acc = o if acc is None else acc + o
    x = x_ref[0]
    x2 = _layer_norm(ALPHA * x + m(5) * acc, lng_ref[1:2, :], lnb_ref[1:2, :])

    o_ref[0] = _ffn_sublayer(x2, m(6), m(7), m(8), w_in_ref, w_out_ref, lng_ref[2:3, :], lnb_ref[2:3, :])


def _mix_ffn(x, mod, za, yf, yb, fr, zc, zu, zv, consts, tt):
    nb, n, _ = x.shape
    tm_spec = pl.BlockSpec((tt, GROUP_W), lambda b, i: (i, b))
    hb = tt // POOL_HALO
    n_hb = n // POOL_HALO
    prev_spec = pl.BlockSpec((POOL_HALO, GROUP_W), lambda b, i: (jnp.maximum(i * hb - 1, 0), b))
    next_spec = pl.BlockSpec((POOL_HALO, GROUP_W), lambda b, i: (jnp.minimum((i + 1) * hb, n_hb - 1), b))
    x_spec = pl.BlockSpec((1, tt, D_MODEL), lambda b, i: (b, i, 0))
    in_specs = [x_spec, pl.BlockSpec((1, N_MOD, D_MODEL), lambda b, i: (b, 0, 0)),
                tm_spec, tm_spec, tm_spec, tm_spec, tm_spec, prev_spec, next_spec, tm_spec, tm_spec]
    in_specs += [_const_spec(c.shape) for c in consts]
    dft_scale = 1.0 / math.sqrt(n * FNET_CH)
    return pl.pallas_call(
        functools.partial(_mix_ffn_kernel, n, tt, dft_scale),
        grid=(nb, n // tt),
        in_specs=in_specs,
        out_specs=x_spec,
        out_shape=jax.ShapeDtypeStruct(x.shape, F32),
        compiler_params=pltpu.CompilerParams(
            dimension_semantics=("parallel", "parallel"), vmem_limit_bytes=VMEM_LIMIT),
        name="mix_ffn",
    )(x, mod, za, yf, yb, fr, zc, zc, zc, zu, zv, *consts)


def _block_diag(w):
    h, c, d = w.shape
    eye = jnp.eye(h, dtype=w.dtype)
    return jnp.einsum("gcd,gk->gckd", w, eye).reshape(h * c, h * d)


def _channel_dft():
    k = np.arange(FNET_CH)
    ang = 2.0 * np.pi * ((k[:, None] * k[None]) % FNET_CH) / FNET_CH
    eye = np.eye(FNET_HEADS)
    fc = np.kron(eye, np.cos(ang))
    fs = np.kron(eye, np.sin(ang))
    return jnp.asarray(np.concatenate([fc, -fs], axis=1), BF16)


def _s5_params(lam_re, lam_im, log_dt, b_re, b_im, c_re, c_im):
    lam = lax.complex(lam_re.astype(F32), lam_im.astype(F32))
    dt = jnp.exp(log_dt.astype(F32))[..., None]
    a_bar = jnp.exp(lam * dt)
    b_bar = ((a_bar - 1.0) / lam)[..., None] * lax.complex(b_re.astype(F32), b_im.astype(F32))
    eye = jnp.eye(SSM_GROUPS, dtype=F32)
    to_b = lambda w: jnp.einsum("dgph,gk->dghkp", w, eye).reshape(2, GROUP_W, N_STATE)
    bmat = jnp.concatenate([to_b(jnp.real(b_bar)), to_b(jnp.imag(b_bar))], axis=2).astype(BF16)
    to_c = lambda w: jnp.einsum("dghp,gk->dgpkh", w, eye).reshape(2, N_STATE, GROUP_W)
    cmat = jnp.concatenate([to_c(c_re.astype(F32)), -to_c(c_im.astype(F32))], axis=1).astype(BF16)
    a_re = jnp.real(a_bar).reshape(2, N_STATE)
    a_im = jnp.imag(a_bar).reshape(2, N_STATE)
    a_vec = jnp.stack([a_re[0], a_im[0], a_re[1], a_im[1]])
    return bmat, cmat, a_vec


def _sincos_2d(n_tokens):
    rows = n_tokens // GRID_W
    quarter = D_MODEL // 4
    omega = 1.0 / (10000.0 ** (jnp.arange(quarter, dtype=F32) / quarter))
    ang_r = jnp.arange(rows, dtype=F32)[:, None] * omega
    ang_c = jnp.arange(GRID_W, dtype=F32)[:, None] * omega
    emb_r = jnp.concatenate([jnp.sin(ang_r), jnp.cos(ang_r)], -1)
    emb_c = jnp.concatenate([jnp.sin(ang_c), jnp.cos(ang_c)], -1)
    half = D_MODEL // 2
    pos = jnp.concatenate([jnp.broadcast_to(emb_r[:, None], (rows, GRID_W, half)),
                           jnp.broadcast_to(emb_c[None], (rows, GRID_W, half))], -1)
    return pos.reshape(rows * GRID_W, D_MODEL)


def _pack_state(st_re, st_im):
    nb = st_re.shape[0]
    re = jnp.transpose(st_re.reshape(nb, 2, N_STATE), (1, 0, 2))
    im = jnp.transpose(st_im.reshape(nb, 2, N_STATE), (1, 0, 2))
    return jnp.concatenate([re, im], axis=2).astype(F32)


def _unpack_state(fin):
    nb = fin.shape[1]
    re = jnp.transpose(fin[:, :, :N_STATE], (1, 0, 2)).reshape(nb, 2, SSM_GROUPS, SSM_STATE)
    im = jnp.transpose(fin[:, :, N_STATE:], (1, 0, 2)).reshape(nb, 2, SSM_GROUPS, SSM_STATE)
    return re, im


def _tiles(nb, n):
    return min(n, 256), 512 // nb


def kernel(x_prompt, x_sample, c, state_s5_re, state_s5_im, c_ctx, w_ada, b_ada, ffn_w_in, ffn_w_out,
           w_mix_in, w_mix_out, mix_norm_g, ssm_lam_re, ssm_lam_im, ssm_log_dt, ssm_b_re, ssm_b_im,
           ssm_c_re, ssm_c_im, ssm_d, ssm_glu_w, ssm_glu_b, fnet_w, pool_w, pool_scale, sgu_w, sgu_b,
           ln_g, ln_b):
    nb_p, n_p, _ = x_prompt.shape
    nb_s, n_s, _ = x_sample.shape

    cond = jnp.concatenate([c.astype(F32), c_ctx.astype(F32)[None],
                            jnp.zeros((16 - nb_s - 1, D_MODEL), F32)], axis=0)
    mod_all = _modulation(cond, w_ada.astype(F32), b_ada.astype(F32))
    mod_all = mod_all.reshape(DEPTH, 16, N_MOD, D_MODEL)

    fcs = _channel_dft()
    ones_blk = jnp.asarray(np.kron(np.eye(SGU_HEADS), np.full((SGU_CH, SGU_CH), 1.0 / SGU_CH)), BF16)
    tabs = {n: _dft_tables(n) for n in sorted({n_p, n_s})}
    pos = _sincos_2d(n_s).astype(F32)

    groups = [
        dict(x=x_prompt.astype(F32), nb=nb_p, n=n_p, zero_state=True),
        dict(x=x_sample.astype(F32), nb=nb_s, n=n_s, zero_state=False),
    ]
    new_re, new_im = [], []
    for i in range(DEPTH):
        w_in = ffn_w_in[i].astype(BF16)
        w_out = ffn_w_out[i].astype(BF16)
        w_mix = w_mix_in[i].astype(BF16)
        lng = ln_g[i].astype(F32)
        lnb = ln_b[i].astype(F32)
        bmat, cmat, a_vec = _s5_params(ssm_lam_re[i], ssm_lam_im[i], ssm_log_dt[i], ssm_b_re[i],
                                       ssm_b_im[i], ssm_c_re[i], ssm_c_im[i])
        sgu_cat = jnp.transpose(sgu_w[i].astype(F32), (1, 0, 2)).reshape(SGU_CHUNK, SGU_HEADS * SGU_CHUNK)
        sgu_bias = jnp.repeat(jnp.transpose(sgu_b[i].astype(F32)), SGU_CH, axis=1)
        consts = [
            ssm_d[i].astype(F32).reshape(1, GROUP_W), ssm_glu_w[i].astype(BF16),
            ssm_glu_b[i].astype(F32).reshape(1, GROUP_W),
            _block_diag(fnet_w[i].astype(F32)).astype(BF16), _block_diag(pool_w[i].astype(F32)).astype(BF16),
            pool_scale[i].astype(F32).reshape(1, GROUP_W), ones_blk, sgu_cat.astype(BF16), sgu_bias,
            mix_norm_g[i].astype(F32).reshape(1, N_MIXERS * GROUP_W), w_mix_out[i].astype(BF16),
            w_in[1], w_out[1], lng, lnb,
        ]
        for grp in groups:
            nb, n = grp["nb"], grp["n"]
            tt, ts = _tiles(nb, n)
            if grp["zero_state"]:
                mod = jnp.broadcast_to(mod_all[i, nb_s][None], (nb, N_MOD, D_MODEL))
                h0 = jnp.zeros((2, nb, 2 * N_STATE), F32)
            else:
                mod = mod_all[i, :nb]
                h0 = _pack_state(state_s5_re[:, i], state_s5_im[:, i])
            use_pos = pos if (i == 0 and not grp["zero_state"]) else None
            x1, za, gre, gim, zc, zu, zv = _ffn_mixin(grp["x"], use_pos, mod, w_in[0], w_out[0], lng, lnb,
                                                      w_mix, fcs, tt)
            yf, yb, fin = _s5_scan(za.reshape(n * nb, GROUP_W), bmat, cmat, a_vec, h0, nb, n, ts)
            ctab, stab = tabs[n]
            fr = _seq_dft(ctab, stab, gre, gim)
            grp["x"] = _mix_ffn(x1, mod, za, yf.reshape(n, nb * GROUP_W), yb.reshape(n, nb * GROUP_W),
                                fr, zc, zu, zv, consts, tt)
            if grp["zero_state"]:
                fre, fim = _unpack_state(fin)
                new_re.append(fre)
                new_im.append(fim)
    return (groups[0]["x"], groups[1]["x"], jnp.stack(new_re, axis=1), jnp.stack(new_im, axis=1))
```

```python
import functools
import math

import numpy as np
import jax
import jax.numpy as jnp
from jax import lax
from jax.experimental import pallas as pl
from jax.experimental.pallas import tpu as pltpu

F32 = jnp.float32
BF16 = jnp.bfloat16

D_MODEL = 1024
DEPTH = 4
GRID_W = 64
GROUP_W = 256
N_MIXERS = 4
SSM_CH = 16
SSM_GROUPS = 16
SSM_STATE = 64
N_STATE = SSM_GROUPS * SSM_STATE
FNET_HEADS = 4
FNET_CH = 64
POOL_CH = 64
SGU_HEADS = 4
SGU_CH = 64
SGU_CHUNK = 128
D_FF = 2816
IN_COLS = 5 * GROUP_W
N_MOD = 9
ALPHA = (2 * DEPTH) ** 0.25
LN_EPS = 1e-5

LANES = 128
FF_CHUNK = 1408
POOL_HALO = 8
DFT_RADIX = 64
VMEM_LIMIT = 56 * 1024 * 1024


def _dot(a, b):
    return jnp.dot(a, b, preferred_element_type=F32)


def _sigmoid(x):
    return 1.0 / (1.0 + jnp.exp(-x))


def _gelu(x):
    return 0.5 * x * (1.0 + jnp.tanh(math.sqrt(2.0 / math.pi) * (x + 0.044715 * (x * x * x))))


def _layer_norm(r, g, b):
    mu = jnp.mean(r, axis=-1, keepdims=True)
    d = r - mu
    var = jnp.mean(d * d, axis=-1, keepdims=True)
    return d * lax.rsqrt(var + LN_EPS) * g + b


def _swiglu(h, w_in_ref, w_out_ref):
    acc = None
    for c in range(D_FF // FF_CHUNK):
        lo = c * FF_CHUNK
        a = _dot(h, w_in_ref[:, lo:lo + FF_CHUNK])
        g = _dot(h, w_in_ref[:, D_FF + lo:D_FF + lo + FF_CHUNK])
        act = (g * _sigmoid(g) * a).astype(BF16)
        o = _dot(act, w_out_ref[lo:lo + FF_CHUNK, :])
        acc = o if acc is None else acc + o
    return acc


def _ffn_sublayer(x, shift, scale, gate, w_in_ref, w_out_ref, g, b):
    h = (x * (1.0 + scale) + shift).astype(BF16)
    f = _swiglu(h, w_in_ref, w_out_ref)
    return _layer_norm(ALPHA * x + 0.5 * gate * f, g, b)


def _mod_kernel(c_ref, w_ref, b_ref, o_ref):
    c = c_ref[...]
    s = c * _sigmoid(c)
    o_ref[0] = jnp.dot(s, w_ref[0], preferred_element_type=F32,
                       precision=lax.Precision.HIGHEST) + b_ref[0]


def _modulation(cond, w_ada, b_ada):
    rows = cond.shape[0]
    tn = 1536
    n_col = N_MOD * D_MODEL
    return pl.pallas_call(
        _mod_kernel,
        grid=(DEPTH, n_col // tn),
        in_specs=[
            pl.BlockSpec((rows, D_MODEL), lambda l, j: (0, 0)),
            pl.BlockSpec((1, D_MODEL, tn), lambda l, j: (l, 0, j)),
            pl.BlockSpec((1, 1, tn), lambda l, j: (l, 0, j)),
        ],
        out_specs=pl.BlockSpec((1, rows, tn), lambda l, j: (l, 0, j)),
        out_shape=jax.ShapeDtypeStruct((DEPTH, rows, n_col), F32),
        compiler_params=pltpu.CompilerParams(
            dimension_semantics=("parallel", "parallel"), vmem_limit_bytes=VMEM_LIMIT),
        name="adaln_modulation",
    )(cond, w_ada, b_ada.reshape(DEPTH, 1, n_col))


def _dft_table_kernel(ca_ref, sa_ref, cb_ref, sb_ref, c_ref, s_ref):
    ca = ca_ref[0]
    sa = sa_ref[0]
    cb = cb_ref[...]
    sb = sb_ref[...]
    c_ref[...] = (ca * cb - sa * sb).astype(BF16)
    s_ref[...] = (sa * cb + ca * sb).astype(BF16)


def _dft_tables(n):
    r = DFT_RADIX
    k = np.arange(n, dtype=np.int64)
    ang_a = 2.0 * np.pi * ((r * np.arange(n // r, dtype=np.int64)[:, None] * k[None]) % n) / n
    ang_b = 2.0 * np.pi * ((np.arange(r, dtype=np.int64)[:, None] * k[None]) % n) / n
    ca = jnp.asarray(np.cos(ang_a), F32).reshape(n // r, 1, n)
    sa = jnp.asarray(np.sin(ang_a), F32).reshape(n // r, 1, n)
    cb = jnp.asarray(np.cos(ang_b), F32)
    sb = jnp.asarray(np.sin(ang_b), F32)
    return pl.pallas_call(
        _dft_table_kernel,
        grid=(n // r,),
        in_specs=[
            pl.BlockSpec((1, 1, n), lambda i: (i, 0, 0)),
            pl.BlockSpec((1, 1, n), lambda i: (i, 0, 0)),
            pl.BlockSpec((r, n), lambda i: (0, 0)),
            pl.BlockSpec((r, n), lambda i: (0, 0)),
        ],
        out_specs=[pl.BlockSpec((r, n), lambda i: (i, 0)), pl.BlockSpec((r, n), lambda i: (i, 0))],
        out_shape=[jax.ShapeDtypeStruct((n, n), BF16), jax.ShapeDtypeStruct((n, n), BF16)],
        compiler_params=pltpu.CompilerParams(
            dimension_semantics=("parallel",), vmem_limit_bytes=VMEM_LIMIT),
        name="dft_tables",
    )(ca, sa, cb, sb)


def _ffn_mixin_kernel(has_pos, *refs):
    if has_pos:
        x_ref, pos_ref = refs[0], refs[1]
        refs = refs[2:]
        x = x_ref[0] + pos_ref[...]
    else:
        x_ref = refs[0]
        refs = refs[1:]
        x = x_ref[0]
    (mod_ref, w_in_ref, w_out_ref, lng_ref, lnb_ref, wmix_ref, fcs_ref,
     x1_ref, za_ref, gre_ref, gim_ref, zc_ref, zu_ref, zv_ref) = refs
    m = lambda k: mod_ref[0, k:k + 1, :]
    x1 = _ffn_sublayer(x, m(0), m(1), m(2), w_in_ref, w_out_ref, lng_ref[0:1, :], lnb_ref[0:1, :])
    x1_ref[0] = x1
    hm = (x1 * (1.0 + m(4)) + m(3)).astype(BF16)
    z = _dot(hm, wmix_ref[...])
    za_ref[...] = z[:, 0:GROUP_W]
    g = _dot(z[:, GROUP_W:2 * GROUP_W].astype(BF16), fcs_ref[...])
    gre_ref[...] = g[:, 0:GROUP_W].astype(BF16)
    gim_ref[...] = g[:, GROUP_W:2 * GROUP_W].astype(BF16)
    zc_ref[...] = z[:, 2 * GROUP_W:3 * GROUP_W]
    zu_ref[...] = z[:, 3 * GROUP_W:4 * GROUP_W]
    zv_ref[...] = z[:, 4 * GROUP_W:5 * GROUP_W]


def _const_spec(shape):
    nd = len(shape)
    return pl.BlockSpec(shape, lambda b, i: (0,) * nd, pipeline_mode=pl.Buffered(1))


def _stacked_spec(shape, *lead):
    nd = len(shape) - len(lead)
    return pl.BlockSpec((None,) * len(lead) + tuple(shape[len(lead):]),
                        lambda b, i: tuple(lead) + (0,) * nd, pipeline_mode=pl.Buffered(1))


def _ffn_mixin(x, pos, mod, w_in, w_out, ln_g, ln_b, w_mix, fcs, layer, tt):
    nb, n, _ = x.shape
    has_pos = pos is not None
    tm_spec = pl.BlockSpec((tt, GROUP_W), lambda b, i: (i, b))
    in_specs = [pl.BlockSpec((1, tt, D_MODEL), lambda b, i: (b, i, 0))]
    args = [x]
    if has_pos:
        in_specs.append(pl.BlockSpec((tt, D_MODEL), lambda b, i: (i, 0)))
        args.append(pos)
    in_specs += [
        pl.BlockSpec((1, N_MOD, D_MODEL), lambda b, i: (b, 0, 0)),
        _stacked_spec(w_in.shape, layer, 0), _stacked_spec(w_out.shape, layer, 0),
        _stacked_spec(ln_g.shape, layer), _stacked_spec(ln_b.shape, layer),
        _stacked_spec(w_mix.shape, layer), _const_spec(fcs.shape),
    ]
    args += [mod, w_in, w_out, ln_g, ln_b, w_mix, fcs]
    tm_shape = lambda dt: jax.ShapeDtypeStruct((n, nb * GROUP_W), dt)
    return pl.pallas_call(
        functools.partial(_ffn_mixin_kernel, has_pos),
        grid=(nb, n // tt),
        in_specs=in_specs,
        out_specs=[pl.BlockSpec((1, tt, D_MODEL), lambda b, i: (b, i, 0))] + [tm_spec] * 6,
        out_shape=[jax.ShapeDtypeStruct(x.shape, F32), tm_shape(F32), tm_shape(BF16), tm_shape(BF16),
                   tm_shape(F32), tm_shape(F32), tm_shape(F32)],
        compiler_params=pltpu.CompilerParams(
            dimension_semantics=("parallel", "parallel"), vmem_limit_bytes=VMEM_LIMIT),
        name="ffn_mixin",
    )(*args)


def _interleave(z_ref, slab_ref, nb, tt):
    for b in range(nb):
        for j in range(GROUP_W // LANES):
            lo = b * GROUP_W + j * LANES
            slab_ref[j, pl.ds(b, tt, stride=nb), :] = z_ref[:, lo:lo + LANES]
    return jnp.concatenate([slab_ref[j] for j in range(GROUP_W // LANES)], axis=1)


def _deinterleave(y, slab_ref, o_ref, nb, tt):
    for j in range(GROUP_W // LANES):
        slab_ref[j] = y[:, j * LANES:(j + 1) * LANES]
    for b in range(nb):
        for j in range(GROUP_W // LANES):
            lo = b * GROUP_W + j * LANES
            o_ref[:, lo:lo + LANES] = slab_ref[j, pl.ds(b, tt, stride=nb), :]


def _s5_scan_kernel(nb, tt, zf_ref, zb_ref, bmat_ref, cmat_ref, a_ref, h0_ref,
                    yf_ref, yb_ref, fin_ref, hsf_ref, hsb_ref, st_ref, slab_ref):
    i = pl.program_id(0)

    @pl.when(i == 0)
    def _():
        st_ref[...] = h0_ref[...]

    hsf_ref[...] = _dot(_interleave(zf_ref, slab_ref.at[0], nb, tt).astype(BF16), bmat_ref[0])
    hsb_ref[...] = _dot(_interleave(zb_ref, slab_ref.at[1], nb, tt).astype(BF16), bmat_ref[1])

    cw = 4096 // nb
    for c in range(N_STATE // cw):
        re = slice(c * cw, (c + 1) * cw)
        im = slice(N_STATE + c * cw, N_STATE + (c + 1) * cw)
        afr = jnp.broadcast_to(a_ref[0:1, re], (nb, cw))
        afi = jnp.broadcast_to(a_ref[1:2, re], (nb, cw))
        abr = jnp.broadcast_to(a_ref[2:3, re], (nb, cw))
        abi = jnp.broadcast_to(a_ref[3:4, re], (nb, cw))

        def step(t, carry, re=re, im=im, afr=afr, afi=afi, abr=abr, abi=abi):
            hfr, hfi, hbr, hbi = carry
            rf = pl.multiple_of(t * nb, nb)
            rb = pl.multiple_of((tt - 1 - t) * nb, nb)
            nfr = afr * hfr - afi * hfi + hsf_ref[pl.ds(rf, nb), re]
            nfi = afr * hfi + afi * hfr + hsf_ref[pl.ds(rf, nb), im]
            nbr = abr * hbr - abi * hbi + hsb_ref[pl.ds(rb, nb), re]
            nbi = abr * hbi + abi * hbr + hsb_ref[pl.ds(rb, nb), im]
            hsf_ref[pl.ds(rf, nb), re] = nfr
            hsf_ref[pl.ds(rf, nb), im] = nfi
            hsb_ref[pl.ds(rb, nb), re] = nbr
            hsb_ref[pl.ds(rb, nb), im] = nbi
            return nfr, nfi, nbr, nbi

        init = (st_ref[0, :, re], st_ref[0, :, im], st_ref[1, :, re], st_ref[1, :, im])
        hfr, hfi, hbr, hbi = lax.fori_loop(0, tt, step, init, unroll=4)
        st_ref[0, :, re] = hfr
        st_ref[0, :, im] = hfi
        st_ref[1, :, re] = hbr
        st_ref[1, :, im] = hbi

    _deinterleave(_dot(hsf_ref[...].astype(BF16), cmat_ref[0]), slab_ref.at[2], yf_ref, nb, tt)
    _deinterleave(_dot(hsb_ref[...].astype(BF16), cmat_ref[1]), slab_ref.at[3], yb_ref, nb, tt)
    fin_ref[...] = st_ref[...]


def _s5_scan(za, bmat, cmat, a_vec, h0, nb, n, tt):
    rows = tt * nb
    nt = n // tt
    cs = lambda shape: pl.BlockSpec(shape, lambda i: (0,) * len(shape))
    return pl.pallas_call(
        functools.partial(_s5_scan_kernel, nb, tt),
        grid=(nt,),
        in_specs=[
            pl.BlockSpec((tt, nb * GROUP_W), lambda i: (i, 0)),
            pl.BlockSpec((tt, nb * GROUP_W), lambda i: (nt - 1 - i, 0)),
            cs(bmat.shape), cs(cmat.shape), cs(a_vec.shape), cs(h0.shape),
        ],
        out_specs=[
            pl.BlockSpec((tt, nb * GROUP_W), lambda i: (i, 0)),
            pl.BlockSpec((tt, nb * GROUP_W), lambda i: (nt - 1 - i, 0)),
            cs(h0.shape),
        ],
        out_shape=[jax.ShapeDtypeStruct(za.shape, F32), jax.ShapeDtypeStruct(za.shape, F32),
                   jax.ShapeDtypeStruct(h0.shape, F32)],
        scratch_shapes=[pltpu.VMEM((rows, 2 * N_STATE), F32), pltpu.VMEM((rows, 2 * N_STATE), F32),
                        pltpu.VMEM(h0.shape, F32),
                        pltpu.VMEM((4, GROUP_W // LANES, rows, LANES), F32)],
        compiler_params=pltpu.CompilerParams(
            dimension_semantics=("arbitrary",), vmem_limit_bytes=VMEM_LIMIT),
        name="s5_scan",
    )(za, za, bmat, cmat, a_vec, h0)


def _seq_dft_kernel(c_ref, s_ref, gre_ref, gim_ref, o_ref):
    part = _dot(c_ref[...], gre_ref[...]) + _dot(s_ref[...], gim_ref[...])

    @pl.when(pl.program_id(1) == 0)
    def _():
        o_ref[...] = part

    @pl.when(pl.program_id(1) > 0)
    def _():
        o_ref[...] += part


def _seq_dft(ctab, stab, gre, gim):
    n, cols = gre.shape
    tl = min(n, 1024)
    tk = min(n, 512)
    return pl.pallas_call(
        _seq_dft_kernel,
        grid=(n // tl, n // tk),
        in_specs=[
            pl.BlockSpec((tl, tk), lambda i, k: (i, k)),
            pl.BlockSpec((tl, tk), lambda i, k: (i, k)),
            pl.BlockSpec((tk, cols), lambda i, k: (k, 0)),
            pl.BlockSpec((tk, cols), lambda i, k: (k, 0)),
        ],
        out_specs=pl.BlockSpec((tl, cols), lambda i, k: (i, 0)),
        out_shape=jax.ShapeDtypeStruct((n, cols), F32),
        compiler_params=pltpu.CompilerParams(
            dimension_semantics=("parallel", "arbitrary"), vmem_limit_bytes=VMEM_LIMIT),
        name="seq_dft",
    )(ctab, stab, gre, gim)


def _seg_mean(x, ones_ref):
    hi = x.astype(BF16)
    lo = (x - hi.astype(F32)).astype(BF16)
    return _dot(hi, ones_ref[...]) + _dot(lo, ones_ref[...])


def _group_norm(y, g):
    return y * lax.rsqrt(jnp.mean(y * y, axis=-1, keepdims=True) + LN_EPS) * g


def _mix_ffn_kernel(n, tt, dft_scale,
                    x_ref, mod_ref, za_ref, yf_ref, yb_ref, fr_ref, zc_ref, zcp_ref, zcn_ref, zu_ref, zv_ref,
                    dskip_ref, gluw_ref, glub_ref, fnw_ref, plw_ref, pls_ref, ones_ref, sguw_ref, sgub_ref,
                    mng_ref, wout_ref, w_in_ref, w_out_ref, lng_ref, lnb_ref, o_ref):
    i = pl.program_id(1)
    last = pl.num_programs(1) - 1
    m = lambda k: mod_ref[0, k:k + 1, :]
    lane_grp = lax.broadcasted_iota(jnp.int32, (1, GROUP_W), 1) // POOL_CH

    ya = za_ref[...] * dskip_ref[...] + yf_ref[...] + yb_ref[...]
    ya = _gelu(ya)
    ya = ya * _sigmoid(_dot(ya.astype(BF16), gluw_ref[...]) + glub_ref[...])

    yb = _dot((fr_ref[...] * dft_scale).astype(BF16), fnw_ref[...])

    zc = zc_ref[...]
    prev = jnp.where(i > 0, zcp_ref[...], 0.0)
    nxt = jnp.where(i < last, zcn_ref[...], 0.0)
    padded = jnp.concatenate([prev, zc, nxt], axis=0)
    rows = tt + 2 * POOL_HALO
    ahead = lambda v, k: pltpu.roll(v, rows - k, axis=0)
    s2 = padded + ahead(padded, 1)
    s4 = s2 + ahead(s2, 2)
    s8 = s4 + ahead(s4, 4)
    s16 = s8 + ahead(s8, 8)
    win = jnp.where(lane_grp == 0, ahead(s2, 7)[:tt],
                    jnp.where(lane_grp == 1, ahead(s4, 6)[:tt],
                              jnp.where(lane_grp == 2, ahead(s8, 4)[:tt], s16[:tt])))
    t_glob = i * tt + lax.broadcasted_iota(jnp.int32, (tt, GROUP_W), 0)
    half = jnp.left_shift(1, lane_grp)
    cnt = jnp.minimum(t_glob + half, n) - jnp.maximum(t_glob - half, 0)
    pooled = win / cnt.astype(F32)
    yc = _dot((pooled - zc).astype(BF16), plw_ref[...]) * pls_ref[...]

    u = _gelu(zu_ref[...])
    v = _gelu(zv_ref[...])
    dv = v - _seg_mean(v, ones_ref)
    vn = dv * lax.rsqrt(_seg_mean(dv * dv, ones_ref) + LN_EPS)
    parts = []
    for c in range(tt // SGU_CHUNK):
        vc = vn[c * SGU_CHUNK:(c + 1) * SGU_CHUNK]
        stacked = jnp.concatenate(
            [jnp.where(lane_grp == g, vc, 0.0).astype(BF16) for g in range(SGU_HEADS)], axis=0)
        parts.append(_dot(sguw_ref[...], stacked) + sgub_ref[...])
    yd = u * jnp.concatenate(parts, axis=0)

    acc = None
    for k, y in enumerate((ya, yb, yc, yd)):
        lo = k * GROUP_W
        yn = _group_norm(y, mng_ref[:, lo:lo + GROUP_W]).astype(BF16)
        o = _dot(yn, wout_ref[lo:lo + GROUP_W, :])
        acc = o if acc is None else acc + o
    x = x_ref[0]
    x2 = _layer_norm(ALPHA * x + m(5) * acc, lng_ref[1:2, :], lnb_ref[1:2, :])

    o_ref[0] = _ffn_sublayer(x2, m(6), m(7), m(8), w_in_ref, w_out_ref, lng_ref[2:3, :], lnb_ref[2:3, :])


def _mix_ffn(x, mod, za, yf, yb, fr, zc, zu, zv, consts, tt):
    nb, n, _ = x.shape
    tm_spec = pl.BlockSpec((tt, GROUP_W), lambda b, i: (i, b))
    hb = tt // POOL_HALO
    n_hb = n // POOL_HALO
    prev_spec = pl.BlockSpec((POOL_HALO, GROUP_W), lambda b, i: (jnp.maximum(i * hb - 1, 0), b))
    next_spec = pl.BlockSpec((POOL_HALO, GROUP_W), lambda b, i: (jnp.minimum((i + 1) * hb, n_hb - 1), b))
    x_spec = pl.BlockSpec((1, tt, D_MODEL), lambda b, i: (b, i, 0))
    in_specs = [x_spec, pl.BlockSpec((1, N_MOD, D_MODEL), lambda b, i: (b, 0, 0)),
                tm_spec, tm_spec, tm_spec, tm_spec, tm_spec, prev_spec, next_spec, tm_spec, tm_spec]
    in_specs += [spec for _, spec in consts]
    consts = [arr for arr, _ in consts]
    dft_scale = 1.0 / math.sqrt(n * FNET_CH)
    return pl.pallas_call(
        functools.partial(_mix_ffn_kernel, n, tt, dft_scale),
        grid=(nb, n // tt),
        in_specs=in_specs,
        out_specs=x_spec,
        out_shape=jax.ShapeDtypeStruct(x.shape, F32),
        compiler_params=pltpu.CompilerParams(
            dimension_semantics=("parallel", "parallel"), vmem_limit_bytes=VMEM_LIMIT),
        name="mix_ffn",
    )(x, mod, za, yf, yb, fr, zc, zc, zc, zu, zv, *consts)


def _block_diag(w):
    h, c, d = w.shape
    eye = jnp.eye(h, dtype=w.dtype)
    return jnp.einsum("gcd,gk->gckd", w, eye).reshape(h * c, h * d)


def _channel_dft():
    k = np.arange(FNET_CH)
    ang = 2.0 * np.pi * ((k[:, None] * k[None]) % FNET_CH) / FNET_CH
    eye = np.eye(FNET_HEADS)
    fc = np.kron(eye, np.cos(ang))
    fs = np.kron(eye, np.sin(ang))
    return jnp.asarray(np.concatenate([fc, -fs], axis=1), BF16)


def _s5_params(lam_re, lam_im, log_dt, b_re, b_im, c_re, c_im):
    lam = lax.complex(lam_re.astype(F32), lam_im.astype(F32))
    dt = jnp.exp(log_dt.astype(F32))[..., None]
    a_bar = jnp.exp(lam * dt)
    b_bar = ((a_bar - 1.0) / lam)[..., None] * lax.complex(b_re.astype(F32), b_im.astype(F32))
    eye = jnp.eye(SSM_GROUPS, dtype=F32)
    to_b = lambda w: jnp.einsum("dgph,gk->dghkp", w, eye).reshape(2, GROUP_W, N_STATE)
    bmat = jnp.concatenate([to_b(jnp.real(b_bar)), to_b(jnp.imag(b_bar))], axis=2).astype(BF16)
    to_c = lambda w: jnp.einsum("dghp,gk->dgpkh", w, eye).reshape(2, N_STATE, GROUP_W)
    cmat = jnp.concatenate([to_c(c_re.astype(F32)), -to_c(c_im.astype(F32))], axis=1).astype(BF16)
    a_re = jnp.real(a_bar).reshape(2, N_STATE)
    a_im = jnp.imag(a_bar).reshape(2, N_STATE)
    a_vec = jnp.stack([a_re[0], a_im[0], a_re[1], a_im[1]])
    return bmat, cmat, a_vec


def _sincos_2d(n_tokens):
    rows = n_tokens // GRID_W
    quarter = D_MODEL // 4
    omega = 1.0 / (10000.0 ** (jnp.arange(quarter, dtype=F32) / quarter))
    ang_r = jnp.arange(rows, dtype=F32)[:, None] * omega
    ang_c = jnp.arange(GRID_W, dtype=F32)[:, None] * omega
    emb_r = jnp.concatenate([jnp.sin(ang_r), jnp.cos(ang_r)], -1)
    emb_c = jnp.concatenate([jnp.sin(ang_c), jnp.cos(ang_c)], -1)
    half = D_MODEL // 2
    pos = jnp.concatenate([jnp.broadcast_to(emb_r[:, None], (rows, GRID_W, half)),
                           jnp.broadcast_to(emb_c[None], (rows, GRID_W, half))], -1)
    return pos.reshape(rows * GRID_W, D_MODEL)


def _pack_state(st_re, st_im):
    nb = st_re.shape[0]
    re = jnp.transpose(st_re.reshape(nb, 2, N_STATE), (1, 0, 2))
    im = jnp.transpose(st_im.reshape(nb, 2, N_STATE), (1, 0, 2))
    return jnp.concatenate([re, im], axis=2).astype(F32)


def _unpack_state(fin):
    nb = fin.shape[1]
    re = jnp.transpose(fin[:, :, :N_STATE], (1, 0, 2)).reshape(nb, 2, SSM_GROUPS, SSM_STATE)
    im = jnp.transpose(fin[:, :, N_STATE:], (1, 0, 2)).reshape(nb, 2, SSM_GROUPS, SSM_STATE)
    return re, im


def _tiles(nb, n):
    return min(n, 256), 512 // nb


def kernel(x_prompt, x_sample, c, state_s5_re, state_s5_im, c_ctx, w_ada, b_ada, ffn_w_in, ffn_w_out,
           w_mix_in, w_mix_out, mix_norm_g, ssm_lam_re, ssm_lam_im, ssm_log_dt, ssm_b_re, ssm_b_im,
           ssm_c_re, ssm_c_im, ssm_d, ssm_glu_w, ssm_glu_b, fnet_w, pool_w, pool_scale, sgu_w, sgu_b,
           ln_g, ln_b):
    nb_p, n_p, _ = x_prompt.shape
    nb_s, n_s, _ = x_sample.shape

    cond = jnp.concatenate([c.astype(F32), c_ctx.astype(F32)[None],
                            jnp.zeros((16 - nb_s - 1, D_MODEL), F32)], axis=0)
    mod_all = _modulation(cond, w_ada.astype(F32), b_ada.astype(F32))
    mod_all = mod_all.reshape(DEPTH, 16, N_MOD, D_MODEL)

    fcs = _channel_dft()
    ones_blk = jnp.asarray(np.kron(np.eye(SGU_HEADS), np.full((SGU_CH, SGU_CH), 1.0 / SGU_CH)), BF16)
    tabs = {n: _dft_tables(n) for n in sorted({n_p, n_s})}
    pos = _sincos_2d(n_s).astype(F32)

    groups = [
        dict(x=x_prompt.astype(F32), nb=nb_p, n=n_p, zero_state=True),
        dict(x=x_sample.astype(F32), nb=nb_s, n=n_s, zero_state=False),
    ]
    new_re, new_im = [], []
    w_in = ffn_w_in.astype(BF16)
    w_out = ffn_w_out.astype(BF16)
    w_mix = w_mix_in.astype(BF16)
    w_mix_o = w_mix_out.astype(BF16)
    lng = ln_g.astype(F32)
    lnb = ln_b.astype(F32)
    for i in range(DEPTH):
        bmat, cmat, a_vec = _s5_params(ssm_lam_re[i], ssm_lam_im[i], ssm_log_dt[i], ssm_b_re[i],
                                       ssm_b_im[i], ssm_c_re[i], ssm_c_im[i])
        sgu_cat = jnp.transpose(sgu_w[i].astype(F32), (1, 0, 2)).reshape(SGU_CHUNK, SGU_HEADS * SGU_CHUNK)
        sgu_bias = jnp.repeat(jnp.transpose(sgu_b[i].astype(F32)), SGU_CH, axis=1)
        small = [
            ssm_d[i].astype(F32).reshape(1, GROUP_W), ssm_glu_w[i].astype(BF16),
            ssm_glu_b[i].astype(F32).reshape(1, GROUP_W),
            _block_diag(fnet_w[i].astype(F32)).astype(BF16), _block_diag(pool_w[i].astype(F32)).astype(BF16),
            pool_scale[i].astype(F32).reshape(1, GROUP_W), ones_blk, sgu_cat.astype(BF16), sgu_bias,
            mix_norm_g[i].astype(F32).reshape(1, N_MIXERS * GROUP_W),
        ]
        consts = [(a, _const_spec(a.shape)) for a in small] + [
            (w_mix_o, _stacked_spec(w_mix_o.shape, i)),
            (w_in, _stacked_spec(w_in.shape, i, 1)), (w_out, _stacked_spec(w_out.shape, i, 1)),
            (lng, _stacked_spec(lng.shape, i)), (lnb, _stacked_spec(lnb.shape, i)),
        ]
        for grp in groups:
            nb, n = grp["nb"], grp["n"]
            tt, ts = _tiles(nb, n)
            if grp["zero_state"]:
                mod = jnp.broadcast_to(mod_all[i, nb_s][None], (nb, N_MOD, D_MODEL))
                h0 = jnp.zeros((2, nb, 2 * N_STATE), F32)
            else:
                mod = mod_all[i, :nb]
                h0 = _pack_state(state_s5_re[:, i], state_s5_im[:, i])
            use_pos = pos if (i == 0 and not grp["zero_state"]) else None
            x1, za, gre, gim, zc, zu, zv = _ffn_mixin(grp["x"], use_pos, mod, w_in, w_out, lng, lnb,
                                                      w_mix, fcs, i, tt)
            yf, yb, fin = _s5_scan(za, bmat, cmat, a_vec, h0, nb, n, ts)
            ctab, stab = tabs[n]
            fr = _seq_dft(ctab, stab, gre, gim)
            grp["x"] = _mix_ffn(x1, mod, za, yf, yb, fr, zc, zu, zv, consts, tt)
            if grp["zero_state"]:
                fre, fim = _unpack_state(fin)
                new_re.append(fre)
                new_im.append(fim)
    return (groups[0]["x"], groups[1]["x"], jnp.stack(new_re, axis=1), jnp.stack(new_im, axis=1))
```

```python
import functools
import math

import numpy as np
import jax
import jax.numpy as jnp
from jax import lax
from jax.experimental import pallas as pl
from jax.experimental.pallas import tpu as pltpu

F32 = jnp.float32
BF16 = jnp.bfloat16

D_MODEL = 1024
DEPTH = 4
GRID_W = 64
GROUP_W = 256
N_MIXERS = 4
SSM_CH = 16
SSM_GROUPS = 16
SSM_STATE = 64
N_STATE = SSM_GROUPS * SSM_STATE
FNET_HEADS = 4
FNET_CH = 64
POOL_CH = 64
SGU_HEADS = 4
SGU_CH = 64
SGU_CHUNK = 128
D_FF = 2816
IN_COLS = 5 * GROUP_W
N_MOD = 9
ALPHA = (2 * DEPTH) ** 0.25
LN_EPS = 1e-5

LANES = 128
MXU_DIM = 256
FF_CHUNKS = (6 * MXU_DIM, 5 * MXU_DIM)
SUB_ROWS = 256
POOL_HALO = 8
DFT_RADIX = 64
VMEM_LIMIT = 56 * 1024 * 1024


def _dot(a, b):
    return jnp.dot(a, b, preferred_element_type=F32)


def _sigmoid(x):
    return 1.0 / (1.0 + jnp.exp(-x))


def _gelu(x):
    return 0.5 * x * (1.0 + jnp.tanh(math.sqrt(2.0 / math.pi) * (x + 0.044715 * (x * x * x))))


def _layer_norm(r, g, b):
    mu = jnp.mean(r, axis=-1, keepdims=True)
    d = r - mu
    var = jnp.mean(d * d, axis=-1, keepdims=True)
    return d * lax.rsqrt(var + LN_EPS) * g + b


def _swiglu(h, w_in_ref, w_out_ref):
    acc = None
    lo = 0
    for width in FF_CHUNKS:
        a = _dot(h, w_in_ref[:, lo:lo + width])
        g = _dot(h, w_in_ref[:, D_FF + lo:D_FF + lo + width])
        act = (g * _sigmoid(g) * a).astype(BF16)
        o = _dot(act, w_out_ref[lo:lo + width, :])
        acc = o if acc is None else acc + o
        lo += width
    return acc


def _ffn_sublayer(x, shift, scale, gate, w_in_ref, w_out_ref, g, b):
    h = (x * (1.0 + scale) + shift).astype(BF16)
    f = _swiglu(h, w_in_ref, w_out_ref)
    return _layer_norm(ALPHA * x + 0.5 * gate * f, g, b)


def _mod_kernel(c_ref, w_ref, b_ref, o_ref):
    c = c_ref[...]
    s = c * _sigmoid(c)
    o_ref[0] = jnp.dot(s, w_ref[0], preferred_element_type=F32,
                       precision=lax.Precision.HIGHEST) + b_ref[0]


def _modulation(cond, w_ada, b_ada):
    rows = cond.shape[0]
    tn = 1536
    n_col = N_MOD * D_MODEL
    return pl.pallas_call(
        _mod_kernel,
        grid=(DEPTH, n_col // tn),
        in_specs=[
            pl.BlockSpec((rows, D_MODEL), lambda l, j: (0, 0)),
            pl.BlockSpec((1, D_MODEL, tn), lambda l, j: (l, 0, j)),
            pl.BlockSpec((1, 1, tn), lambda l, j: (l, 0, j)),
        ],
        out_specs=pl.BlockSpec((1, rows, tn), lambda l, j: (l, 0, j)),
        out_shape=jax.ShapeDtypeStruct((DEPTH, rows, n_col), F32),
        compiler_params=pltpu.CompilerParams(
            dimension_semantics=("parallel", "parallel"), vmem_limit_bytes=VMEM_LIMIT),
        name="adaln_modulation",
    )(cond, w_ada, b_ada.reshape(DEPTH, 1, n_col))


def _dft_table_kernel(ca_ref, sa_ref, cb_ref, sb_ref, c_ref, s_ref):
    ca = ca_ref[0]
    sa = sa_ref[0]
    cb = cb_ref[...]
    sb = sb_ref[...]
    c_ref[...] = (ca * cb - sa * sb).astype(BF16)
    s_ref[...] = (sa * cb + ca * sb).astype(BF16)


def _dft_tables(n):
    r = DFT_RADIX
    k = np.arange(n, dtype=np.int64)
    ang_a = 2.0 * np.pi * ((r * np.arange(n // r, dtype=np.int64)[:, None] * k[None]) % n) / n
    ang_b = 2.0 * np.pi * ((np.arange(r, dtype=np.int64)[:, None] * k[None]) % n) / n
    ca = jnp.asarray(np.cos(ang_a), F32).reshape(n // r, 1, n)
    sa = jnp.asarray(np.sin(ang_a), F32).reshape(n // r, 1, n)
    cb = jnp.asarray(np.cos(ang_b), F32)
    sb = jnp.asarray(np.sin(ang_b), F32)
    return pl.pallas_call(
        _dft_table_kernel,
        grid=(n // r,),
        in_specs=[
            pl.BlockSpec((1, 1, n), lambda i: (i, 0, 0)),
            pl.BlockSpec((1, 1, n), lambda i: (i, 0, 0)),
            pl.BlockSpec((r, n), lambda i: (0, 0)),
            pl.BlockSpec((r, n), lambda i: (0, 0)),
        ],
        out_specs=[pl.BlockSpec((r, n), lambda i: (i, 0)), pl.BlockSpec((r, n), lambda i: (i, 0))],
        out_shape=[jax.ShapeDtypeStruct((n, n), BF16), jax.ShapeDtypeStruct((n, n), BF16)],
        compiler_params=pltpu.CompilerParams(
            dimension_semantics=("parallel",), vmem_limit_bytes=VMEM_LIMIT),
        name="dft_tables",
    )(ca, sa, cb, sb)


def _ffn_mixin_kernel(has_pos, tt, *refs):
    if has_pos:
        x_ref, pos_ref = refs[0], refs[1]
        refs = refs[2:]
    else:
        x_ref, pos_ref = refs[0], None
        refs = refs[1:]
    (mod_ref, w_in_ref, w_out_ref, lng_ref, lnb_ref, wmix_ref, fcs_ref,
     x1_ref, za_ref, gre_ref, gim_ref, zc_ref, zu_ref, zv_ref) = refs
    m = lambda k: mod_ref[0, k:k + 1, :]
    for r0 in range(0, tt, SUB_ROWS):
        rows = slice(r0, r0 + SUB_ROWS)
        x = x_ref[0, rows, :]
        if pos_ref is not None:
            x = x + pos_ref[rows, :]
        x1 = _ffn_sublayer(x, m(0), m(1), m(2), w_in_ref, w_out_ref, lng_ref[0:1, :], lnb_ref[0:1, :])
        x1_ref[0, rows, :] = x1
        hm = (x1 * (1.0 + m(4)) + m(3)).astype(BF16)
        z = _dot(hm, wmix_ref[...])
        za_ref[rows, :] = z[:, 0:GROUP_W]
        g = _dot(z[:, GROUP_W:2 * GROUP_W].astype(BF16), fcs_ref[...])
        gre_ref[rows, :] = g[:, 0:GROUP_W].astype(BF16)
        gim_ref[rows, :] = g[:, GROUP_W:2 * GROUP_W].astype(BF16)
        zc_ref[rows, :] = z[:, 2 * GROUP_W:3 * GROUP_W]
        zu_ref[rows, :] = z[:, 3 * GROUP_W:4 * GROUP_W]
        zv_ref[rows, :] = z[:, 4 * GROUP_W:5 * GROUP_W]


def _const_spec(shape):
    nd = len(shape)
    return pl.BlockSpec(shape, lambda b, i: (0,) * nd, pipeline_mode=pl.Buffered(1))


def _stacked_spec(shape, *lead):
    nd = len(shape) - len(lead)
    return pl.BlockSpec((None,) * len(lead) + tuple(shape[len(lead):]),
                        lambda b, i: tuple(lead) + (0,) * nd, pipeline_mode=pl.Buffered(1))


def _ffn_mixin(x, pos, mod, w_in, w_out, ln_g, ln_b, w_mix, fcs, layer, tt):
    nb, n, _ = x.shape
    has_pos = pos is not None
    tm_spec = pl.BlockSpec((tt, GROUP_W), lambda b, i: (i, b))
    in_specs = [pl.BlockSpec((1, tt, D_MODEL), lambda b, i: (b, i, 0))]
    args = [x]
    if has_pos:
        in_specs.append(pl.BlockSpec((tt, D_MODEL), lambda b, i: (i, 0)))
        args.append(pos)
    in_specs += [
        pl.BlockSpec((1, N_MOD, D_MODEL), lambda b, i: (b, 0, 0)),
        _stacked_spec(w_in.shape, layer, 0), _stacked_spec(w_out.shape, layer, 0),
        _stacked_spec(ln_g.shape, layer), _stacked_spec(ln_b.shape, layer),
        _stacked_spec(w_mix.shape, layer), _const_spec(fcs.shape),
    ]
    args += [mod, w_in, w_out, ln_g, ln_b, w_mix, fcs]
    tm_shape = lambda dt: jax.ShapeDtypeStruct((n, nb * GROUP_W), dt)
    return pl.pallas_call(
        functools.partial(_ffn_mixin_kernel, has_pos, tt),
        grid=(nb, n // tt),
        in_specs=in_specs,
        out_specs=[pl.BlockSpec((1, tt, D_MODEL), lambda b, i: (b, i, 0))] + [tm_spec] * 6,
        out_shape=[jax.ShapeDtypeStruct(x.shape, F32), tm_shape(F32), tm_shape(BF16), tm_shape(BF16),
                   tm_shape(F32), tm_shape(F32), tm_shape(F32)],
        compiler_params=pltpu.CompilerParams(
            dimension_semantics=("parallel", "parallel"), vmem_limit_bytes=VMEM_LIMIT),
        name="ffn_mixin",
    )(*args)


def _interleave(z_ref, slab_ref, nb, tt):
    for b in range(nb):
        for j in range(GROUP_W // LANES):
            lo = b * GROUP_W + j * LANES
            slab_ref[j, pl.ds(b, tt, stride=nb), :] = z_ref[:, lo:lo + LANES]
    return jnp.concatenate([slab_ref[j] for j in range(GROUP_W // LANES)], axis=1)


def _deinterleave(y, slab_ref, o_ref, nb, tt):
    for j in range(GROUP_W // LANES):
        slab_ref[j] = y[:, j * LANES:(j + 1) * LANES]
    for b in range(nb):
        for j in range(GROUP_W // LANES):
            lo = b * GROUP_W + j * LANES
            o_ref[:, lo:lo + LANES] = slab_ref[j, pl.ds(b, tt, stride=nb), :]


def _s5_scan_kernel(nb, tt, zf_ref, zb_ref, bmat_ref, cmat_ref, a_ref, h0_ref,
                    yf_ref, yb_ref, fin_ref, hsf_ref, hsb_ref, st_ref, slab_ref):
    i = pl.program_id(0)

    @pl.when(i == 0)
    def _():
        st_ref[...] = h0_ref[...]

    cw = min(MXU_DIM, 2048 // nb)
    for d, (z_ref, hs_ref, y_ref) in enumerate(((zf_ref, hsf_ref, yf_ref), (zb_ref, hsb_ref, yb_ref))):
        u = _interleave(z_ref, slab_ref.at[d], nb, tt).astype(BF16)
        y = None
        for c in range(N_STATE // cw):
            re = slice(c * cw, (c + 1) * cw)
            im = slice(N_STATE + c * cw, N_STATE + (c + 1) * cw)
            hs_ref[:, re] = _dot(u, bmat_ref[d, :, re])
            hs_ref[:, im] = _dot(u, bmat_ref[d, :, im])
            ar = jnp.broadcast_to(a_ref[2 * d:2 * d + 1, re], (nb, cw))
            ai = jnp.broadcast_to(a_ref[2 * d + 1:2 * d + 2, re], (nb, cw))
            hr = st_ref[d, :, re]
            hi = st_ref[d, :, im]
            for t in range(tt):
                r0 = (t if d == 0 else tt - 1 - t) * nb
                hr, hi = (ar * hr - ai * hi + hs_ref[r0:r0 + nb, re],
                          ar * hi + ai * hr + hs_ref[r0:r0 + nb, im])
                hs_ref[r0:r0 + nb, re] = hr
                hs_ref[r0:r0 + nb, im] = hi
            st_ref[d, :, re] = hr
            st_ref[d, :, im] = hi
            part = (_dot(hs_ref[:, re].astype(BF16), cmat_ref[d, re, :])
                    + _dot(hs_ref[:, im].astype(BF16), cmat_ref[d, im, :]))
            y = part if y is None else y + part
        _deinterleave(y, slab_ref.at[2 + d], y_ref, nb, tt)
    fin_ref[...] = st_ref[...]


def _s5_scan(za, bmat, cmat, a_vec, h0, nb, n, tt):
    rows = tt * nb
    nt = n // tt
    cs = lambda shape: pl.BlockSpec(shape, lambda i: (0,) * len(shape))
    return pl.pallas_call(
        functools.partial(_s5_scan_kernel, nb, tt),
        grid=(nt,),
        in_specs=[
            pl.BlockSpec((tt, nb * GROUP_W), lambda i: (i, 0)),
            pl.BlockSpec((tt, nb * GROUP_W), lambda i: (nt - 1 - i, 0)),
            cs(bmat.shape), cs(cmat.shape), cs(a_vec.shape), cs(h0.shape),
        ],
        out_specs=[
            pl.BlockSpec((tt, nb * GROUP_W), lambda i: (i, 0)),
            pl.BlockSpec((tt, nb * GROUP_W), lambda i: (nt - 1 - i, 0)),
            cs(h0.shape),
        ],
        out_shape=[jax.ShapeDtypeStruct(za.shape, F32), jax.ShapeDtypeStruct(za.shape, F32),
                   jax.ShapeDtypeStruct(h0.shape, F32)],
        scratch_shapes=[pltpu.VMEM((rows, 2 * N_STATE), F32), pltpu.VMEM((rows, 2 * N_STATE), F32),
                        pltpu.VMEM(h0.shape, F32),
                        pltpu.VMEM((4, GROUP_W // LANES, rows, LANES), F32)],
        compiler_params=pltpu.CompilerParams(
            dimension_semantics=("arbitrary",), vmem_limit_bytes=VMEM_LIMIT),
        name="s5_scan",
    )(za, za, bmat, cmat, a_vec, h0)


def _seq_dft_kernel(c_ref, s_ref, gre_ref, gim_ref, o_ref):
    part = _dot(c_ref[...], gre_ref[...]) + _dot(s_ref[...], gim_ref[...])

    @pl.when(pl.program_id(1) == 0)
    def _():
        o_ref[...] = part

    @pl.when(pl.program_id(1) > 0)
    def _():
        o_ref[...] += part


def _seq_dft(ctab, stab, gre, gim):
    n, cols = gre.shape
    tl = min(n, 1024)
    tk = min(n, 512)
    return pl.pallas_call(
        _seq_dft_kernel,
        grid=(n // tl, n // tk),
        in_specs=[
            pl.BlockSpec((tl, tk), lambda i, k: (i, k)),
            pl.BlockSpec((tl, tk), lambda i, k: (i, k)),
            pl.BlockSpec((tk, cols), lambda i, k: (k, 0)),
            pl.BlockSpec((tk, cols), lambda i, k: (k, 0)),
        ],
        out_specs=pl.BlockSpec((tl, cols), lambda i, k: (i, 0)),
        out_shape=jax.ShapeDtypeStruct((n, cols), F32),
        compiler_params=pltpu.CompilerParams(
            dimension_semantics=("parallel", "arbitrary"), vmem_limit_bytes=VMEM_LIMIT),
        name="seq_dft",
    )(ctab, stab, gre, gim)


def _seg_mean(x, ones_ref):
    hi = x.astype(BF16)
    lo = (x - hi.astype(F32)).astype(BF16)
    return _dot(hi, ones_ref[...]) + _dot(lo, ones_ref[...])


def _group_norm(y, g):
    return y * lax.rsqrt(jnp.mean(y * y, axis=-1, keepdims=True) + LN_EPS) * g


def _mix_ffn_kernel(n, tt, dft_scale,
                    x_ref, mod_ref, za_ref, yf_ref, yb_ref, fr_ref, zc_ref, zcp_ref, zcn_ref, zu_ref, zv_ref,
                    dskip_ref, gluw_ref, glub_ref, fnw_ref, plw_ref, pls_ref, ones_ref, sguw_ref, sgub_ref,
                    mng_ref, wout_ref, w_in_ref, w_out_ref, lng_ref, lnb_ref, o_ref):
    i = pl.program_id(1)
    last = pl.num_programs(1) - 1
    m = lambda k: mod_ref[0, k:k + 1, :]
    lane_grp = lax.broadcasted_iota(jnp.int32, (1, GROUP_W), 1) // POOL_CH
    half = jnp.left_shift(1, lane_grp)
    sub = min(tt, SUB_ROWS)
    prows = sub + 2 * POOL_HALO
    ahead = lambda v, k: pltpu.roll(v, prows - k, axis=0)

    for r0 in range(0, tt, sub):
        rows = slice(r0, r0 + sub)

        ya = za_ref[rows, :] * dskip_ref[...] + yf_ref[rows, :] + yb_ref[rows, :]
        ya = _gelu(ya)
        ya = ya * _sigmoid(_dot(ya.astype(BF16), gluw_ref[...]) + glub_ref[...])

        yb = _dot((fr_ref[rows, :] * dft_scale).astype(BF16), fnw_ref[...])

        zc = zc_ref[rows, :]
        if r0 == 0:
            prev = jnp.where(i > 0, zcp_ref[...], 0.0)
        else:
            prev = zc_ref[r0 - POOL_HALO:r0, :]
        if r0 + sub == tt:
            nxt = jnp.where(i < last, zcn_ref[...], 0.0)
        else:
            nxt = zc_ref[r0 + sub:r0 + sub + POOL_HALO, :]
        padded = jnp.concatenate([prev, zc, nxt], axis=0)
        s2 = padded + ahead(padded, 1)
        s4 = s2 + ahead(s2, 2)
        s8 = s4 + ahead(s4, 4)
        s16 = s8 + ahead(s8, 8)
        win = jnp.where(lane_grp == 0, ahead(s2, 7)[:sub],
                        jnp.where(lane_grp == 1, ahead(s4, 6)[:sub],
                                  jnp.where(lane_grp == 2, ahead(s8, 4)[:sub], s16[:sub])))
        t_glob = i * tt + r0 + lax.broadcasted_iota(jnp.int32, (sub, GROUP_W), 0)
        cnt = jnp.minimum(t_glob + half, n) - jnp.maximum(t_glob - half, 0)
        pooled = win / cnt.astype(F32)
        yc = _dot((pooled - zc).astype(BF16), plw_ref[...]) * pls_ref[...]

        u = _gelu(zu_ref[rows, :])
        v = _gelu(zv_ref[rows, :])
        dv = v - _seg_mean(v, ones_ref)
        vn = dv * lax.rsqrt(_seg_mean(dv * dv, ones_ref) + LN_EPS)
        parts = []
        for c in range(sub // SGU_CHUNK):
            vc = vn[c * SGU_CHUNK:(c + 1) * SGU_CHUNK]
            stacked = jnp.concatenate(
                [jnp.where(lane_grp == g, vc, 0.0).astype(BF16) for g in range(SGU_HEADS)], axis=0)
            parts.append(_dot(sguw_ref[...], stacked) + sgub_ref[...])
        yd = u * jnp.concatenate(parts, axis=0)

        acc = None
        for k, y in enumerate((ya, yb, yc, yd)):
            lo = k * GROUP_W
            yn = _group_norm(y, mng_ref[:, lo:lo + GROUP_W]).astype(BF16)
            o = _dot(yn, wout_ref[lo:lo + GROUP_W, :])
            acc = o if acc is None else acc + o
        x = x_ref[0, rows, :]
        x2 = _layer_norm(ALPHA * x + m(5) * acc, lng_ref[1:2, :], lnb_ref[1:2, :])

        o_ref[0, rows, :] = _ffn_sublayer(x2, m(6), m(7), m(8), w_in_ref, w_out_ref,
                                          lng_ref[2:3, :], lnb_ref[2:3, :])


def _mix_ffn(x, mod, za, yf, yb, fr, zc, zu, zv, consts, tt):
    nb, n, _ = x.shape
    tm_spec = pl.BlockSpec((tt, GROUP_W), lambda b, i: (i, b))
    hb = tt // POOL_HALO
    n_hb = n // POOL_HALO
    prev_spec = pl.BlockSpec((POOL_HALO, GROUP_W), lambda b, i: (jnp.maximum(i * hb - 1, 0), b))
    next_spec = pl.BlockSpec((POOL_HALO, GROUP_W), lambda b, i: (jnp.minimum((i + 1) * hb, n_hb - 1), b))
    x_spec = pl.BlockSpec((1, tt, D_MODEL), lambda b, i: (b, i, 0))
    in_specs = [x_spec, pl.BlockSpec((1, N_MOD, D_MODEL), lambda b, i: (b, 0, 0)),
                tm_spec, tm_spec, tm_spec, tm_spec, tm_spec, prev_spec, next_spec, tm_spec, tm_spec]
    in_specs += [spec for _, spec in consts]
    consts = [arr for arr, _ in consts]
    dft_scale = 1.0 / math.sqrt(n * FNET_CH)
    return pl.pallas_call(
        functools.partial(_mix_ffn_kernel, n, tt, dft_scale),
        grid=(nb, n // tt),
        in_specs=in_specs,
        out_specs=x_spec,
        out_shape=jax.ShapeDtypeStruct(x.shape, F32),
        compiler_params=pltpu.CompilerParams(
            dimension_semantics=("parallel", "parallel"), vmem_limit_bytes=VMEM_LIMIT),
        name="mix_ffn",
    )(x, mod, za, yf, yb, fr, zc, zc, zc, zu, zv, *consts)


def _block_diag(w):
    h, c, d = w.shape
    eye = jnp.eye(h, dtype=w.dtype)
    return jnp.einsum("gcd,gk->gckd", w, eye).reshape(h * c, h * d)


def _channel_dft():
    k = np.arange(FNET_CH)
    ang = 2.0 * np.pi * ((k[:, None] * k[None]) % FNET_CH) / FNET_CH
    eye = np.eye(FNET_HEADS)
    fc = np.kron(eye, np.cos(ang))
    fs = np.kron(eye, np.sin(ang))
    return jnp.asarray(np.concatenate([fc, -fs], axis=1), BF16)


def _s5_params(lam_re, lam_im, log_dt, b_re, b_im, c_re, c_im):
    lam = lax.complex(lam_re.astype(F32), lam_im.astype(F32))
    dt = jnp.exp(log_dt.astype(F32))[..., None]
    a_bar = jnp.exp(lam * dt)
    b_bar = ((a_bar - 1.0) / lam)[..., None] * lax.complex(b_re.astype(F32), b_im.astype(F32))
    eye = jnp.eye(SSM_GROUPS, dtype=F32)
    to_b = lambda w: jnp.einsum("dgph,gk->dghkp", w, eye).reshape(2, GROUP_W, N_STATE)
    bmat = jnp.concatenate([to_b(jnp.real(b_bar)), to_b(jnp.imag(b_bar))], axis=2).astype(BF16)
    to_c = lambda w: jnp.einsum("dghp,gk->dgpkh", w, eye).reshape(2, N_STATE, GROUP_W)
    cmat = jnp.concatenate([to_c(c_re.astype(F32)), -to_c(c_im.astype(F32))], axis=1).astype(BF16)
    a_re = jnp.real(a_bar).reshape(2, N_STATE)
    a_im = jnp.imag(a_bar).reshape(2, N_STATE)
    a_vec = jnp.stack([a_re[0], a_im[0], a_re[1], a_im[1]])
    return bmat, cmat, a_vec


def _sincos_2d(n_tokens):
    rows = n_tokens // GRID_W
    quarter = D_MODEL // 4
    omega = 1.0 / (10000.0 ** (jnp.arange(quarter, dtype=F32) / quarter))
    ang_r = jnp.arange(rows, dtype=F32)[:, None] * omega
    ang_c = jnp.arange(GRID_W, dtype=F32)[:, None] * omega
    emb_r = jnp.concatenate([jnp.sin(ang_r), jnp.cos(ang_r)], -1)
    emb_c = jnp.concatenate([jnp.sin(ang_c), jnp.cos(ang_c)], -1)
    half = D_MODEL // 2
    pos = jnp.concatenate([jnp.broadcast_to(emb_r[:, None], (rows, GRID_W, half)),
                           jnp.broadcast_to(emb_c[None], (rows, GRID_W, half))], -1)
    return pos.reshape(rows * GRID_W, D_MODEL)


def _pack_state(st_re, st_im):
    nb = st_re.shape[0]
    re = jnp.transpose(st_re.reshape(nb, 2, N_STATE), (1, 0, 2))
    im = jnp.transpose(st_im.reshape(nb, 2, N_STATE), (1, 0, 2))
    return jnp.concatenate([re, im], axis=2).astype(F32)


def _unpack_state(fin):
    nb = fin.shape[1]
    re = jnp.transpose(fin[:, :, :N_STATE], (1, 0, 2)).reshape(nb, 2, SSM_GROUPS, SSM_STATE)
    im = jnp.transpose(fin[:, :, N_STATE:], (1, 0, 2)).reshape(nb, 2, SSM_GROUPS, SSM_STATE)
    return re, im


def _tiles(nb, n):
    return min(n, 2 * SUB_ROWS), 512 // nb


def kernel(x_prompt, x_sample, c, state_s5_re, state_s5_im, c_ctx, w_ada, b_ada, ffn_w_in, ffn_w_out,
           w_mix_in, w_mix_out, mix_norm_g, ssm_lam_re, ssm_lam_im, ssm_log_dt, ssm_b_re, ssm_b_im,
           ssm_c_re, ssm_c_im, ssm_d, ssm_glu_w, ssm_glu_b, fnet_w, pool_w, pool_scale, sgu_w, sgu_b,
           ln_g, ln_b):
    nb_p, n_p, _ = x_prompt.shape
    nb_s, n_s, _ = x_sample.shape

    cond = jnp.concatenate([c.astype(F32), c_ctx.astype(F32)[None],
                            jnp.zeros((16 - nb_s - 1, D_MODEL), F32)], axis=0)
    mod_all = _modulation(cond, w_ada.astype(F32), b_ada.astype(F32))
    mod_all = mod_all.reshape(DEPTH, 16, N_MOD, D_MODEL)

    fcs = _channel_dft()
    ones_blk = jnp.asarray(np.kron(np.eye(SGU_HEADS), np.full((SGU_CH, SGU_CH), 1.0 / SGU_CH)), BF16)
    tabs = {n: _dft_tables(n) for n in sorted({n_p, n_s})}
    pos = _sincos_2d(n_s).astype(F32)

    groups = [
        dict(x=x_prompt.astype(F32), nb=nb_p, n=n_p, zero_state=True),
        dict(x=x_sample.astype(F32), nb=nb_s, n=n_s, zero_state=False),
    ]
    new_re, new_im = [], []
    w_in = ffn_w_in.astype(BF16)
    w_out = ffn_w_out.astype(BF16)
    w_mix = w_mix_in.astype(BF16)
    w_mix_o = w_mix_out.astype(BF16)
    lng = ln_g.astype(F32)
    lnb = ln_b.astype(F32)
    for i in range(DEPTH):
        bmat, cmat, a_vec = _s5_params(ssm_lam_re[i], ssm_lam_im[i], ssm_log_dt[i], ssm_b_re[i],
                                       ssm_b_im[i], ssm_c_re[i], ssm_c_im[i])
        sgu_cat = jnp.transpose(sgu_w[i].astype(F32), (1, 0, 2)).reshape(SGU_CHUNK, SGU_HEADS * SGU_CHUNK)
        sgu_bias = jnp.repeat(jnp.transpose(sgu_b[i].astype(F32)), SGU_CH, axis=1)
        small = [
            ssm_d[i].astype(F32).reshape(1, GROUP_W), ssm_glu_w[i].astype(BF16),
            ssm_glu_b[i].astype(F32).reshape(1, GROUP_W),
            _block_diag(fnet_w[i].astype(F32)).astype(BF16), _block_diag(pool_w[i].astype(F32)).astype(BF16),
            pool_scale[i].astype(F32).reshape(1, GROUP_W), ones_blk, sgu_cat.astype(BF16), sgu_bias,
            mix_norm_g[i].astype(F32).reshape(1, N_MIXERS * GROUP_W),
        ]
        consts = [(a, _const_spec(a.shape)) for a in small] + [
            (w_mix_o, _stacked_spec(w_mix_o.shape, i)),
            (w_in, _stacked_spec(w_in.shape, i, 1)), (w_out, _stacked_spec(w_out.shape, i, 1)),
            (lng, _stacked_spec(lng.shape, i)), (lnb, _stacked_spec(lnb.shape, i)),
        ]
        for grp in groups:
            nb, n = grp["nb"], grp["n"]
            tt, ts = _tiles(nb, n)
            if grp["zero_state"]:
                mod = jnp.broadcast_to(mod_all[i, nb_s][None], (nb, N_MOD, D_MODEL))
                h0 = jnp.zeros((2, nb, 2 * N_STATE), F32)
            else:
                mod = mod_all[i, :nb]
                h0 = _pack_state(state_s5_re[:, i], state_s5_im[:, i])
            use_pos = pos if (i == 0 and not grp["zero_state"]) else None
            x1, za, gre, gim, zc, zu, zv = _ffn_mixin(grp["x"], use_pos, mod, w_in, w_out, lng, lnb,
                                                      w_mix, fcs, i, tt)
            yf, yb, fin = _s5_scan(za, bmat, cmat, a_vec, h0, nb, n, ts)
            ctab, stab = tabs[n]
            fr = _seq_dft(ctab, stab, gre, gim)
            grp["x"] = _mix_ffn(x1, mod, za, yf, yb, fr, zc, zu, zv, consts, tt)
            if grp["zero_state"]:
                fre, fim = _unpack_state(fin)
                new_re.append(fre)
                new_im.append(fim)
    return (groups[0]["x"], groups[1]["x"], jnp.stack(new_re, axis=1), jnp.stack(new_im, axis=1))
```

```python
import functools
import math

import numpy as np
import jax
import jax.numpy as jnp
from jax import lax
from jax.experimental import pallas as pl
from jax.experimental.pallas import tpu as pltpu

F32 = jnp.float32
BF16 = jnp.bfloat16

D_MODEL = 1024
DEPTH = 4
GRID_W = 64
GROUP_W = 256
N_MIXERS = 4
SSM_CH = 16
SSM_GROUPS = 16
SSM_STATE = 64
N_STATE = SSM_GROUPS * SSM_STATE
FNET_HEADS = 4
FNET_CH = 64
POOL_CH = 64
SGU_HEADS = 4
SGU_CH = 64
SGU_CHUNK = 128
D_FF = 2816
IN_COLS = 5 * GROUP_W
N_MOD = 9
ALPHA = (2 * DEPTH) ** 0.25
LN_EPS = 1e-5

LANES = 128
MXU_DIM = 256
FF_CHUNKS = (6 * MXU_DIM, 5 * MXU_DIM)
SUB_ROWS = 256
POOL_HALO = 8
DFT_RADIX = 64
VMEM_LIMIT = 56 * 1024 * 1024


def _dot(a, b):
    return jnp.dot(a, b, preferred_element_type=F32)


def _sigmoid(x):
    return 1.0 / (1.0 + jnp.exp(-x))


def _gelu(x):
    return 0.5 * x * (1.0 + jnp.tanh(math.sqrt(2.0 / math.pi) * (x + 0.044715 * (x * x * x))))


def _layer_norm(r, g, b):
    mu = jnp.mean(r, axis=-1, keepdims=True)
    d = r - mu
    var = jnp.mean(d * d, axis=-1, keepdims=True)
    return d * lax.rsqrt(var + LN_EPS) * g + b


def _swiglu(h, w_in_ref, w_out_ref):
    acc = None
    lo = 0
    for width in FF_CHUNKS:
        a = _dot(h, w_in_ref[:, lo:lo + width])
        g = _dot(h, w_in_ref[:, D_FF + lo:D_FF + lo + width])
        act = (g * _sigmoid(g) * a).astype(BF16)
        o = _dot(act, w_out_ref[lo:lo + width, :])
        acc = o if acc is None else acc + o
        lo += width
    return acc


def _ffn_sublayer(x, shift, scale, gate, w_in_ref, w_out_ref, g, b):
    h = (x * (1.0 + scale) + shift).astype(BF16)
    f = _swiglu(h, w_in_ref, w_out_ref)
    return _layer_norm(ALPHA * x + 0.5 * gate * f, g, b)


def _mod_kernel(c_ref, w_ref, b_ref, o_ref):
    c = c_ref[...]
    s = c * _sigmoid(c)
    o_ref[0] = jnp.dot(s, w_ref[0], preferred_element_type=F32,
                       precision=lax.Precision.HIGHEST) + b_ref[0]


def _modulation(cond, w_ada, b_ada):
    rows = cond.shape[0]
    tn = 1536
    n_col = N_MOD * D_MODEL
    return pl.pallas_call(
        _mod_kernel,
        grid=(DEPTH, n_col // tn),
        in_specs=[
            pl.BlockSpec((rows, D_MODEL), lambda l, j: (0, 0)),
            pl.BlockSpec((1, D_MODEL, tn), lambda l, j: (l, 0, j)),
            pl.BlockSpec((1, 1, tn), lambda l, j: (l, 0, j)),
        ],
        out_specs=pl.BlockSpec((1, rows, tn), lambda l, j: (l, 0, j)),
        out_shape=jax.ShapeDtypeStruct((DEPTH, rows, n_col), F32),
        compiler_params=pltpu.CompilerParams(
            dimension_semantics=("parallel", "parallel"), vmem_limit_bytes=VMEM_LIMIT),
        name="adaln_modulation",
    )(cond, w_ada, b_ada.reshape(DEPTH, 1, n_col))


def _dft_table_kernel(ca_ref, sa_ref, cb_ref, sb_ref, c_ref, s_ref):
    ca = ca_ref[0]
    sa = sa_ref[0]
    cb = cb_ref[...]
    sb = sb_ref[...]
    c_ref[...] = (ca * cb - sa * sb).astype(BF16)
    s_ref[...] = (sa * cb + ca * sb).astype(BF16)


def _dft_tables(n):
    r = DFT_RADIX
    k = np.arange(n, dtype=np.int64)
    ang_a = 2.0 * np.pi * ((r * np.arange(n // r, dtype=np.int64)[:, None] * k[None]) % n) / n
    ang_b = 2.0 * np.pi * ((np.arange(r, dtype=np.int64)[:, None] * k[None]) % n) / n
    ca = jnp.asarray(np.cos(ang_a), F32).reshape(n // r, 1, n)
    sa = jnp.asarray(np.sin(ang_a), F32).reshape(n // r, 1, n)
    cb = jnp.asarray(np.cos(ang_b), F32)
    sb = jnp.asarray(np.sin(ang_b), F32)
    return pl.pallas_call(
        _dft_table_kernel,
        grid=(n // r,),
        in_specs=[
            pl.BlockSpec((1, 1, n), lambda i: (i, 0, 0)),
            pl.BlockSpec((1, 1, n), lambda i: (i, 0, 0)),
            pl.BlockSpec((r, n), lambda i: (0, 0)),
            pl.BlockSpec((r, n), lambda i: (0, 0)),
        ],
        out_specs=[pl.BlockSpec((r, n), lambda i: (i, 0)), pl.BlockSpec((r, n), lambda i: (i, 0))],
        out_shape=[jax.ShapeDtypeStruct((n, n), BF16), jax.ShapeDtypeStruct((n, n), BF16)],
        compiler_params=pltpu.CompilerParams(
            dimension_semantics=("parallel",), vmem_limit_bytes=VMEM_LIMIT),
        name="dft_tables",
    )(ca, sa, cb, sb)


def _ffn_mixin_kernel(has_pos, tt, *refs):
    if has_pos:
        x_ref, pos_ref = refs[0], refs[1]
        refs = refs[2:]
    else:
        x_ref, pos_ref = refs[0], None
        refs = refs[1:]
    (mod_ref, w_in_ref, w_out_ref, lng_ref, lnb_ref, wmix_ref, fcs_ref,
     x1_ref, za_ref, gre_ref, gim_ref, zc_ref, zu_ref, zv_ref) = refs
    m = lambda k: mod_ref[0, k:k + 1, :]
    for r0 in range(0, tt, SUB_ROWS):
        rows = slice(r0, r0 + SUB_ROWS)
        x = x_ref[0, rows, :]
        if pos_ref is not None:
            x = x + pos_ref[rows, :]
        x1 = _ffn_sublayer(x, m(0), m(1), m(2), w_in_ref, w_out_ref, lng_ref[0:1, :], lnb_ref[0:1, :])
        x1_ref[0, rows, :] = x1
        hm = (x1 * (1.0 + m(4)) + m(3)).astype(BF16)
        z = _dot(hm, wmix_ref[...])
        za_ref[rows, :] = z[:, 0:GROUP_W]
        g = _dot(z[:, GROUP_W:2 * GROUP_W].astype(BF16), fcs_ref[...])
        gre_ref[rows, :] = g[:, 0:GROUP_W].astype(BF16)
        gim_ref[rows, :] = g[:, GROUP_W:2 * GROUP_W].astype(BF16)
        zc_ref[rows, :] = z[:, 2 * GROUP_W:3 * GROUP_W]
        zu_ref[rows, :] = z[:, 3 * GROUP_W:4 * GROUP_W]
        zv_ref[rows, :] = z[:, 4 * GROUP_W:5 * GROUP_W]


def _const_spec(shape):
    nd = len(shape)
    return pl.BlockSpec(shape, lambda b, i: (0,) * nd, pipeline_mode=pl.Buffered(1))


def _stacked_spec(shape, *lead):
    nd = len(shape) - len(lead)
    return pl.BlockSpec((None,) * len(lead) + tuple(shape[len(lead):]),
                        lambda b, i: tuple(lead) + (0,) * nd, pipeline_mode=pl.Buffered(1))


def _ffn_mixin(x, pos, mod, w_in, w_out, ln_g, ln_b, w_mix, fcs, layer, tt):
    nb, n, _ = x.shape
    has_pos = pos is not None
    tm_spec = pl.BlockSpec((tt, GROUP_W), lambda b, i: (i, b))
    in_specs = [pl.BlockSpec((1, tt, D_MODEL), lambda b, i: (b, i, 0))]
    args = [x]
    if has_pos:
        in_specs.append(pl.BlockSpec((tt, D_MODEL), lambda b, i: (i, 0)))
        args.append(pos)
    in_specs += [
        pl.BlockSpec((1, N_MOD, D_MODEL), lambda b, i: (b, 0, 0)),
        _stacked_spec(w_in.shape, layer, 0), _stacked_spec(w_out.shape, layer, 0),
        _stacked_spec(ln_g.shape, layer), _stacked_spec(ln_b.shape, layer),
        _stacked_spec(w_mix.shape, layer), _const_spec(fcs.shape),
    ]
    args += [mod, w_in, w_out, ln_g, ln_b, w_mix, fcs]
    tm_shape = lambda dt: jax.ShapeDtypeStruct((n, nb * GROUP_W), dt)
    return pl.pallas_call(
        functools.partial(_ffn_mixin_kernel, has_pos, tt),
        grid=(nb, n // tt),
        in_specs=in_specs,
        out_specs=[pl.BlockSpec((1, tt, D_MODEL), lambda b, i: (b, i, 0))] + [tm_spec] * 6,
        out_shape=[jax.ShapeDtypeStruct(x.shape, F32), tm_shape(F32), tm_shape(BF16), tm_shape(BF16),
                   tm_shape(F32), tm_shape(F32), tm_shape(F32)],
        compiler_params=pltpu.CompilerParams(
            dimension_semantics=("parallel", "parallel"), vmem_limit_bytes=VMEM_LIMIT),
        name="ffn_mixin",
    )(*args)


def _interleave(z_ref, slab_ref, nb, tt):
    for b in range(nb):
        for j in range(GROUP_W // LANES):
            lo = b * GROUP_W + j * LANES
            slab_ref[j, pl.ds(b, tt, stride=nb), :] = z_ref[:, lo:lo + LANES]
    return jnp.concatenate([slab_ref[j] for j in range(GROUP_W // LANES)], axis=1)


def _deinterleave(y, slab_ref, o_ref, nb, tt):
    for j in range(GROUP_W // LANES):
        slab_ref[j] = y[:, j * LANES:(j + 1) * LANES]
    for b in range(nb):
        for j in range(GROUP_W // LANES):
            lo = b * GROUP_W + j * LANES
            o_ref[:, lo:lo + LANES] = slab_ref[j, pl.ds(b, tt, stride=nb), :]


def _s5_scan_kernel(nb, tt, zf_ref, zb_ref, bmat_ref, cmat_ref, a_ref, h0_ref,
                    yf_ref, yb_ref, fin_ref, hsf_ref, hsb_ref, st_ref, slab_ref):
    i = pl.program_id(0)

    @pl.when(i == 0)
    def _():
        st_ref[...] = h0_ref[...]

    cw = min(MXU_DIM, 2048 // nb)
    for d, (z_ref, hs_ref, y_ref) in enumerate(((zf_ref, hsf_ref, yf_ref), (zb_ref, hsb_ref, yb_ref))):
        u = _interleave(z_ref, slab_ref.at[d], nb, tt).astype(BF16)
        y = None
        for c in range(N_STATE // cw):
            re = slice(c * cw, (c + 1) * cw)
            im = slice(N_STATE + c * cw, N_STATE + (c + 1) * cw)
            hs_ref[:, re] = _dot(u, bmat_ref[d, :, re])
            hs_ref[:, im] = _dot(u, bmat_ref[d, :, im])
            ar = jnp.broadcast_to(a_ref[2 * d:2 * d + 1, re], (nb, cw))
            ai = jnp.broadcast_to(a_ref[2 * d + 1:2 * d + 2, re], (nb, cw))
            hr = st_ref[d, :, re]
            hi = st_ref[d, :, im]
            for t in range(tt):
                r0 = (t if d == 0 else tt - 1 - t) * nb
                hr, hi = (ar * hr - ai * hi + hs_ref[r0:r0 + nb, re],
                          ar * hi + ai * hr + hs_ref[r0:r0 + nb, im])
                hs_ref[r0:r0 + nb, re] = hr
                hs_ref[r0:r0 + nb, im] = hi
            st_ref[d, :, re] = hr
            st_ref[d, :, im] = hi
            part = (_dot(hs_ref[:, re].astype(BF16), cmat_ref[d, re, :])
                    + _dot(hs_ref[:, im].astype(BF16), cmat_ref[d, im, :]))
            y = part if y is None else y + part
        _deinterleave(y, slab_ref.at[2 + d], y_ref, nb, tt)
    fin_ref[...] = st_ref[...]


def _s5_scan(za, bmat, cmat, a_vec, h0, nb, n, tt):
    rows = tt * nb
    nt = n // tt
    cs = lambda shape: pl.BlockSpec(shape, lambda i: (0,) * len(shape))
    return pl.pallas_call(
        functools.partial(_s5_scan_kernel, nb, tt),
        grid=(nt,),
        in_specs=[
            pl.BlockSpec((tt, nb * GROUP_W), lambda i: (i, 0)),
            pl.BlockSpec((tt, nb * GROUP_W), lambda i: (nt - 1 - i, 0)),
            cs(bmat.shape), cs(cmat.shape), cs(a_vec.shape), cs(h0.shape),
        ],
        out_specs=[
            pl.BlockSpec((tt, nb * GROUP_W), lambda i: (i, 0)),
            pl.BlockSpec((tt, nb * GROUP_W), lambda i: (nt - 1 - i, 0)),
            cs(h0.shape),
        ],
        out_shape=[jax.ShapeDtypeStruct(za.shape, F32), jax.ShapeDtypeStruct(za.shape, F32),
                   jax.ShapeDtypeStruct(h0.shape, F32)],
        scratch_shapes=[pltpu.VMEM((rows, 2 * N_STATE), F32), pltpu.VMEM((rows, 2 * N_STATE), F32),
                        pltpu.VMEM(h0.shape, F32),
                        pltpu.VMEM((4, GROUP_W // LANES, rows, LANES), F32)],
        compiler_params=pltpu.CompilerParams(
            dimension_semantics=("arbitrary",), vmem_limit_bytes=VMEM_LIMIT),
        name="s5_scan",
    )(za, za, bmat, cmat, a_vec, h0)


def _seq_dft_kernel(c_ref, s_ref, gre_ref, gim_ref, o_ref):
    o_ref[...] = _dot(c_ref[...], gre_ref[...]) + _dot(s_ref[...], gim_ref[...])


def _seq_dft(ctab, stab, gre, gim):
    n, cols = gre.shape
    tl = min(n, SUB_ROWS)
    resident = pl.BlockSpec((n, cols), lambda i: (0, 0), pipeline_mode=pl.Buffered(1))
    return pl.pallas_call(
        _seq_dft_kernel,
        grid=(n // tl,),
        in_specs=[
            pl.BlockSpec((tl, n), lambda i: (i, 0)),
            pl.BlockSpec((tl, n), lambda i: (i, 0)),
            resident, resident,
        ],
        out_specs=pl.BlockSpec((tl, cols), lambda i: (i, 0)),
        out_shape=jax.ShapeDtypeStruct((n, cols), F32),
        compiler_params=pltpu.CompilerParams(
            dimension_semantics=("parallel",), vmem_limit_bytes=VMEM_LIMIT),
        name="seq_dft",
    )(ctab, stab, gre, gim)


def _seg_mean(x, ones_ref):
    hi = x.astype(BF16)
    lo = (x - hi.astype(F32)).astype(BF16)
    return _dot(hi, ones_ref[...]) + _dot(lo, ones_ref[...])


def _group_norm(y, g):
    return y * lax.rsqrt(jnp.mean(y * y, axis=-1, keepdims=True) + LN_EPS) * g


def _mix_ffn_kernel(n, tt, dft_scale,
                    x_ref, mod_ref, za_ref, yf_ref, yb_ref, fr_ref, zc_ref, zcp_ref, zcn_ref, zu_ref, zv_ref,
                    dskip_ref, gluw_ref, glub_ref, fnw_ref, plw_ref, pls_ref, ones_ref, sguw_ref, sgub_ref,
                    mng_ref, wout_ref, w_in_ref, w_out_ref, lng_ref, lnb_ref, o_ref):
    i = pl.program_id(1)
    last = pl.num_programs(1) - 1
    m = lambda k: mod_ref[0, k:k + 1, :]
    lane_grp = lax.broadcasted_iota(jnp.int32, (1, GROUP_W), 1) // POOL_CH
    half = jnp.left_shift(1, lane_grp)
    sub = min(tt, SUB_ROWS)
    prows = sub + 2 * POOL_HALO
    ahead = lambda v, k: pltpu.roll(v, prows - k, axis=0)

    for r0 in range(0, tt, sub):
        rows = slice(r0, r0 + sub)

        ya = za_ref[rows, :] * dskip_ref[...] + yf_ref[rows, :] + yb_ref[rows, :]
        ya = _gelu(ya)
        ya = ya * _sigmoid(_dot(ya.astype(BF16), gluw_ref[...]) + glub_ref[...])

        yb = _dot((fr_ref[rows, :] * dft_scale).astype(BF16), fnw_ref[...])

        zc = zc_ref[rows, :]
        if r0 == 0:
            prev = jnp.where(i > 0, zcp_ref[...], 0.0)
        else:
            prev = zc_ref[r0 - POOL_HALO:r0, :]
        if r0 + sub == tt:
            nxt = jnp.where(i < last, zcn_ref[...], 0.0)
        else:
            nxt = zc_ref[r0 + sub:r0 + sub + POOL_HALO, :]
        padded = jnp.concatenate([prev, zc, nxt], axis=0)
        s2 = padded + ahead(padded, 1)
        s4 = s2 + ahead(s2, 2)
        s8 = s4 + ahead(s4, 4)
        s16 = s8 + ahead(s8, 8)
        win = jnp.where(lane_grp == 0, ahead(s2, 7)[:sub],
                        jnp.where(lane_grp == 1, ahead(s4, 6)[:sub],
                                  jnp.where(lane_grp == 2, ahead(s8, 4)[:sub], s16[:sub])))
        t_glob = i * tt + r0 + lax.broadcasted_iota(jnp.int32, (sub, GROUP_W), 0)
        cnt = jnp.minimum(t_glob + half, n) - jnp.maximum(t_glob - half, 0)
        pooled = win / cnt.astype(F32)
        yc = _dot((pooled - zc).astype(BF16), plw_ref[...]) * pls_ref[...]

        u = _gelu(zu_ref[rows, :])
        v = _gelu(zv_ref[rows, :])
        dv = v - _seg_mean(v, ones_ref)
        vn = dv * lax.rsqrt(_seg_mean(dv * dv, ones_ref) + LN_EPS)
        parts = []
        for c in range(sub // SGU_CHUNK):
            vc = vn[c * SGU_CHUNK:(c + 1) * SGU_CHUNK]
            stacked = jnp.concatenate(
                [jnp.where(lane_grp == g, vc, 0.0).astype(BF16) for g in range(SGU_HEADS)], axis=0)
            parts.append(_dot(sguw_ref[...], stacked) + sgub_ref[...])
        yd = u * jnp.concatenate(parts, axis=0)

        acc = None
        for k, y in enumerate((ya, yb, yc, yd)):
            lo = k * GROUP_W
            yn = _group_norm(y, mng_ref[:, lo:lo + GROUP_W]).astype(BF16)
            o = _dot(yn, wout_ref[lo:lo + GROUP_W, :])
            acc = o if acc is None else acc + o
        x = x_ref[0, rows, :]
        x2 = _layer_norm(ALPHA * x + m(5) * acc, lng_ref[1:2, :], lnb_ref[1:2, :])

        o_ref[0, rows, :] = _ffn_sublayer(x2, m(6), m(7), m(8), w_in_ref, w_out_ref,
                                          lng_ref[2:3, :], lnb_ref[2:3, :])


def _mix_ffn(x, mod, za, yf, yb, fr, zc, zu, zv, consts, tt):
    nb, n, _ = x.shape
    tm_spec = pl.BlockSpec((tt, GROUP_W), lambda b, i: (i, b))
    hb = tt // POOL_HALO
    n_hb = n // POOL_HALO
    prev_spec = pl.BlockSpec((POOL_HALO, GROUP_W), lambda b, i: (jnp.maximum(i * hb - 1, 0), b))
    next_spec = pl.BlockSpec((POOL_HALO, GROUP_W), lambda b, i: (jnp.minimum((i + 1) * hb, n_hb - 1), b))
    x_spec = pl.BlockSpec((1, tt, D_MODEL), lambda b, i: (b, i, 0))
    in_specs = [x_spec, pl.BlockSpec((1, N_MOD, D_MODEL), lambda b, i: (b, 0, 0)),
                tm_spec, tm_spec, tm_spec, tm_spec, tm_spec, prev_spec, next_spec, tm_spec, tm_spec]
    in_specs += [spec for _, spec in consts]
    consts = [arr for arr, _ in consts]
    dft_scale = 1.0 / math.sqrt(n * FNET_CH)
    return pl.pallas_call(
        functools.partial(_mix_ffn_kernel, n, tt, dft_scale),
        grid=(nb, n // tt),
        in_specs=in_specs,
        out_specs=x_spec,
        out_shape=jax.ShapeDtypeStruct(x.shape, F32),
        compiler_params=pltpu.CompilerParams(
            dimension_semantics=("parallel", "parallel"), vmem_limit_bytes=VMEM_LIMIT),
        name="mix_ffn",
    )(x, mod, za, yf, yb, fr, zc, zc, zc, zu, zv, *consts)


def _block_diag(w):
    h, c, d = w.shape
    eye = jnp.eye(h, dtype=w.dtype)
    return jnp.einsum("gcd,gk->gckd", w, eye).reshape(h * c, h * d)


def _channel_dft():
    k = np.arange(FNET_CH)
    ang = 2.0 * np.pi * ((k[:, None] * k[None]) % FNET_CH) / FNET_CH
    eye = np.eye(FNET_HEADS)
    fc = np.kron(eye, np.cos(ang))
    fs = np.kron(eye, np.sin(ang))
    return jnp.asarray(np.concatenate([fc, -fs], axis=1), BF16)


def _s5_params(lam_re, lam_im, log_dt, b_re, b_im, c_re, c_im):
    lam = lax.complex(lam_re.astype(F32), lam_im.astype(F32))
    dt = jnp.exp(log_dt.astype(F32))[..., None]
    a_bar = jnp.exp(lam * dt)
    b_bar = ((a_bar - 1.0) / lam)[..., None] * lax.complex(b_re.astype(F32), b_im.astype(F32))
    eye = jnp.eye(SSM_GROUPS, dtype=F32)
    to_b = lambda w: jnp.einsum("dgph,gk->dghkp", w, eye).reshape(2, GROUP_W, N_STATE)
    bmat = jnp.concatenate([to_b(jnp.real(b_bar)), to_b(jnp.imag(b_bar))], axis=2).astype(BF16)
    to_c = lambda w: jnp.einsum("dghp,gk->dgpkh", w, eye).reshape(2, N_STATE, GROUP_W)
    cmat = jnp.concatenate([to_c(c_re.astype(F32)), -to_c(c_im.astype(F32))], axis=1).astype(BF16)
    a_re = jnp.real(a_bar).reshape(2, N_STATE)
    a_im = jnp.imag(a_bar).reshape(2, N_STATE)
    a_vec = jnp.stack([a_re[0], a_im[0], a_re[1], a_im[1]])
    return bmat, cmat, a_vec


def _sincos_2d(n_tokens):
    rows = n_tokens // GRID_W
    quarter = D_MODEL // 4
    omega = 1.0 / (10000.0 ** (jnp.arange(quarter, dtype=F32) / quarter))
    ang_r = jnp.arange(rows, dtype=F32)[:, None] * omega
    ang_c = jnp.arange(GRID_W, dtype=F32)[:, None] * omega
    emb_r = jnp.concatenate([jnp.sin(ang_r), jnp.cos(ang_r)], -1)
    emb_c = jnp.concatenate([jnp.sin(ang_c), jnp.cos(ang_c)], -1)
    half = D_MODEL // 2
    pos = jnp.concatenate([jnp.broadcast_to(emb_r[:, None], (rows, GRID_W, half)),
                           jnp.broadcast_to(emb_c[None], (rows, GRID_W, half))], -1)
    return pos.reshape(rows * GRID_W, D_MODEL)


def _pack_state(st_re, st_im):
    nb = st_re.shape[0]
    re = jnp.transpose(st_re.reshape(nb, 2, N_STATE), (1, 0, 2))
    im = jnp.transpose(st_im.reshape(nb, 2, N_STATE), (1, 0, 2))
    return jnp.concatenate([re, im], axis=2).astype(F32)


def _unpack_state(fin):
    nb = fin.shape[1]
    re = jnp.transpose(fin[:, :, :N_STATE], (1, 0, 2)).reshape(nb, 2, SSM_GROUPS, SSM_STATE)
    im = jnp.transpose(fin[:, :, N_STATE:], (1, 0, 2)).reshape(nb, 2, SSM_GROUPS, SSM_STATE)
    return re, im


def _tiles(nb, n):
    return min(n, 2 * SUB_ROWS), 512 // nb


def kernel(x_prompt, x_sample, c, state_s5_re, state_s5_im, c_ctx, w_ada, b_ada, ffn_w_in, ffn_w_out,
           w_mix_in, w_mix_out, mix_norm_g, ssm_lam_re, ssm_lam_im, ssm_log_dt, ssm_b_re, ssm_b_im,
           ssm_c_re, ssm_c_im, ssm_d, ssm_glu_w, ssm_glu_b, fnet_w, pool_w, pool_scale, sgu_w, sgu_b,
           ln_g, ln_b):
    nb_p, n_p, _ = x_prompt.shape
    nb_s, n_s, _ = x_sample.shape

    cond = jnp.concatenate([c.astype(F32), c_ctx.astype(F32)[None],
                            jnp.zeros((16 - nb_s - 1, D_MODEL), F32)], axis=0)
    mod_all = _modulation(cond, w_ada.astype(F32), b_ada.astype(F32))
    mod_all = mod_all.reshape(DEPTH, 16, N_MOD, D_MODEL)

    fcs = _channel_dft()
    ones_blk = jnp.asarray(np.kron(np.eye(SGU_HEADS), np.full((SGU_CH, SGU_CH), 1.0 / SGU_CH)), BF16)
    tabs = {n: _dft_tables(n) for n in sorted({n_p, n_s})}
    pos = _sincos_2d(n_s).astype(F32)

    groups = [
        dict(x=x_prompt.astype(F32), nb=nb_p, n=n_p, zero_state=True),
        dict(x=x_sample.astype(F32), nb=nb_s, n=n_s, zero_state=False),
    ]
    new_re, new_im = [], []
    w_in = ffn_w_in.astype(BF16)
    w_out = ffn_w_out.astype(BF16)
    w_mix = w_mix_in.astype(BF16)
    w_mix_o = w_mix_out.astype(BF16)
    lng = ln_g.astype(F32)
    lnb = ln_b.astype(F32)
    for i in range(DEPTH):
        bmat, cmat, a_vec = _s5_params(ssm_lam_re[i], ssm_lam_im[i], ssm_log_dt[i], ssm_b_re[i],
                                       ssm_b_im[i], ssm_c_re[i], ssm_c_im[i])
        sgu_cat = jnp.transpose(sgu_w[i].astype(F32), (1, 0, 2)).reshape(SGU_CHUNK, SGU_HEADS * SGU_CHUNK)
        sgu_bias = jnp.repeat(jnp.transpose(sgu_b[i].astype(F32)), SGU_CH, axis=1)
        small = [
            ssm_d[i].astype(F32).reshape(1, GROUP_W), ssm_glu_w[i].astype(BF16),
            ssm_glu_b[i].astype(F32).reshape(1, GROUP_W),
            _block_diag(fnet_w[i].astype(F32)).astype(BF16), _block_diag(pool_w[i].astype(F32)).astype(BF16),
            pool_scale[i].astype(F32).reshape(1, GROUP_W), ones_blk, sgu_cat.astype(BF16), sgu_bias,
            mix_norm_g[i].astype(F32).reshape(1, N_MIXERS * GROUP_W),
        ]
        consts = [(a, _const_spec(a.shape)) for a in small] + [
            (w_mix_o, _stacked_spec(w_mix_o.shape, i)),
            (w_in, _stacked_spec(w_in.shape, i, 1)), (w_out, _stacked_spec(w_out.shape, i, 1)),
            (lng, _stacked_spec(lng.shape, i)), (lnb, _stacked_spec(lnb.shape, i)),
        ]
        for grp in groups:
            nb, n = grp["nb"], grp["n"]
            tt, ts = _tiles(nb, n)
            if grp["zero_state"]:
                mod = jnp.broadcast_to(mod_all[i, nb_s][None], (nb, N_MOD, D_MODEL))
                h0 = jnp.zeros((2, nb, 2 * N_STATE), F32)
            else:
                mod = mod_all[i, :nb]
                h0 = _pack_state(state_s5_re[:, i], state_s5_im[:, i])
            use_pos = pos if (i == 0 and not grp["zero_state"]) else None
            x1, za, gre, gim, zc, zu, zv = _ffn_mixin(grp["x"], use_pos, mod, w_in, w_out, lng, lnb,
                                                      w_mix, fcs, i, tt)
            yf, yb, fin = _s5_scan(za, bmat, cmat, a_vec, h0, nb, n, ts)
            ctab, stab = tabs[n]
            fr = _seq_dft(ctab, stab, gre, gim)
            grp["x"] = _mix_ffn(x1, mod, za, yf, yb, fr, zc, zu, zv, consts, tt)
            if grp["zero_state"]:
                fre, fim = _unpack_state(fin)
                new_re.append(fre)
                new_im.append(fim)
    return (groups[0]["x"], groups[1]["x"], jnp.stack(new_re, axis=1), jnp.stack(new_im, axis=1))
```

```python
import functools
import math

import numpy as np
import jax
import jax.numpy as jnp
from jax import lax
from jax.experimental import pallas as pl
from jax.experimental.pallas import tpu as pltpu

F32 = jnp.float32
BF16 = jnp.bfloat16

D_MODEL = 1024
DEPTH = 4
GRID_W = 64
GROUP_W = 256
N_MIXERS = 4
SSM_CH = 16
SSM_GROUPS = 16
SSM_STATE = 64
N_STATE = SSM_GROUPS * SSM_STATE
FNET_HEADS = 4
FNET_CH = 64
POOL_CH = 64
SGU_HEADS = 4
SGU_CH = 64
SGU_CHUNK = 128
D_FF = 2816
IN_COLS = 5 * GROUP_W
N_MOD = 9
ALPHA = (2 * DEPTH) ** 0.25
LN_EPS = 1e-5

LANES = 128
MXU_DIM = 256
FF_CHUNKS = (6 * MXU_DIM, 5 * MXU_DIM)
SUB_ROWS = 256
POOL_HALO = 8
DFT_RADIX = 64
RADIX4_MIN_LEN = 1024
VMEM_LIMIT = 56 * 1024 * 1024


def _dot(a, b):
    return jnp.dot(a, b, preferred_element_type=F32)


def _sigmoid(x):
    return 1.0 / (1.0 + jnp.exp(-x))


def _gelu(x):
    k = math.sqrt(2.0 / math.pi)
    h = 0.5 * x
    return h + h * jnp.tanh(x * (k + (k * 0.044715) * (x * x)))


def _layer_norm(r, g, b):
    mu = jnp.mean(r, axis=-1, keepdims=True)
    d = r - mu
    var = jnp.mean(d * d, axis=-1, keepdims=True)
    return d * lax.rsqrt(var + LN_EPS) * g + b


def _swiglu(h, w_in_ref, w_out_ref):
    acc = None
    lo = 0
    for width in FF_CHUNKS:
        a = _dot(h, w_in_ref[:, lo:lo + width])
        g = _dot(h, w_in_ref[:, D_FF + lo:D_FF + lo + width])
        act = (g * _sigmoid(g) * a).astype(BF16)
        o = _dot(act, w_out_ref[lo:lo + width, :])
        acc = o if acc is None else acc + o
        lo += width
    return acc


def _ffn_sublayer(x, shift, scale, gate, w_in_ref, w_out_ref, g, b):
    h = (x * (1.0 + scale) + shift).astype(BF16)
    f = _swiglu(h, w_in_ref, w_out_ref)
    return _layer_norm(ALPHA * x + 0.5 * gate * f, g, b)


def _mod_kernel(c_ref, w_ref, b_ref, o_ref):
    c = c_ref[...]
    s = c * _sigmoid(c)
    o_ref[0] = jnp.dot(s, w_ref[0], preferred_element_type=F32,
                       precision=lax.Precision.HIGHEST) + b_ref[0]


def _modulation(cond, w_ada, b_ada):
    rows = cond.shape[0]
    tn = 1536
    n_col = N_MOD * D_MODEL
    return pl.pallas_call(
        _mod_kernel,
        grid=(DEPTH, n_col // tn),
        in_specs=[
            pl.BlockSpec((rows, D_MODEL), lambda l, j: (0, 0)),
            pl.BlockSpec((1, D_MODEL, tn), lambda l, j: (l, 0, j)),
            pl.BlockSpec((1, 1, tn), lambda l, j: (l, 0, j)),
        ],
        out_specs=pl.BlockSpec((1, rows, tn), lambda l, j: (l, 0, j)),
        out_shape=jax.ShapeDtypeStruct((DEPTH, rows, n_col), F32),
        compiler_params=pltpu.CompilerParams(
            dimension_semantics=("parallel", "parallel"), vmem_limit_bytes=VMEM_LIMIT),
        name="adaln_modulation",
    )(cond, w_ada, b_ada.reshape(DEPTH, 1, n_col))


def _dft_table_kernel(ca_ref, sa_ref, cb_ref, sb_ref, c_ref, s_ref):
    ca = ca_ref[0]
    sa = sa_ref[0]
    cb = cb_ref[...]
    sb = sb_ref[...]
    c_ref[...] = (ca * cb - sa * sb).astype(BF16)
    s_ref[...] = (sa * cb + ca * sb).astype(BF16)


def _dft_tables(n):
    r = DFT_RADIX
    k = np.arange(n, dtype=np.int64)
    ang_a = 2.0 * np.pi * ((r * np.arange(n // r, dtype=np.int64)[:, None] * k[None]) % n) / n
    ang_b = 2.0 * np.pi * ((np.arange(r, dtype=np.int64)[:, None] * k[None]) % n) / n
    ca = jnp.asarray(np.cos(ang_a), F32).reshape(n // r, 1, n)
    sa = jnp.asarray(np.sin(ang_a), F32).reshape(n // r, 1, n)
    cb = jnp.asarray(np.cos(ang_b), F32)
    sb = jnp.asarray(np.sin(ang_b), F32)
    return pl.pallas_call(
        _dft_table_kernel,
        grid=(n // r,),
        in_specs=[
            pl.BlockSpec((1, 1, n), lambda i: (i, 0, 0)),
            pl.BlockSpec((1, 1, n), lambda i: (i, 0, 0)),
            pl.BlockSpec((r, n), lambda i: (0, 0)),
            pl.BlockSpec((r, n), lambda i: (0, 0)),
        ],
        out_specs=[pl.BlockSpec((r, n), lambda i: (i, 0)), pl.BlockSpec((r, n), lambda i: (i, 0))],
        out_shape=[jax.ShapeDtypeStruct((n, n), BF16), jax.ShapeDtypeStruct((n, n), BF16)],
        compiler_params=pltpu.CompilerParams(
            dimension_semantics=("parallel",), vmem_limit_bytes=VMEM_LIMIT),
        name="dft_tables",
    )(ca, sa, cb, sb)


def _ffn_mixin_kernel(has_pos, tt, *refs):
    if has_pos:
        x_ref, pos_ref = refs[0], refs[1]
        refs = refs[2:]
    else:
        x_ref, pos_ref = refs[0], None
        refs = refs[1:]
    (mod_ref, w_in_ref, w_out_ref, lng_ref, lnb_ref, wmix_ref, fcs_ref,
     x1_ref, za_ref, gre_ref, gim_ref, zc_ref, zu_ref, zv_ref) = refs
    m = lambda k: mod_ref[0, k:k + 1, :]
    for r0 in range(0, tt, SUB_ROWS):
        rows = slice(r0, r0 + SUB_ROWS)
        x = x_ref[0, rows, :]
        if pos_ref is not None:
            x = x + pos_ref[rows, :]
        x1 = _ffn_sublayer(x, m(0), m(1), m(2), w_in_ref, w_out_ref, lng_ref[0:1, :], lnb_ref[0:1, :])
        x1_ref[0, rows, :] = x1
        hm = (x1 * (1.0 + m(4)) + m(3)).astype(BF16)
        z = _dot(hm, wmix_ref[...])
        za_ref[rows, :] = z[:, 0:GROUP_W]
        g = _dot(z[:, GROUP_W:2 * GROUP_W].astype(BF16), fcs_ref[...])
        gre_ref[rows, :] = g[:, 0:GROUP_W].astype(BF16)
        gim_ref[rows, :] = g[:, GROUP_W:2 * GROUP_W].astype(BF16)
        zc_ref[rows, :] = z[:, 2 * GROUP_W:3 * GROUP_W]
        zu_ref[rows, :] = z[:, 3 * GROUP_W:4 * GROUP_W]
        zv_ref[rows, :] = z[:, 4 * GROUP_W:5 * GROUP_W]


def _const_spec(shape):
    nd = len(shape)
    return pl.BlockSpec(shape, lambda b, i: (0,) * nd, pipeline_mode=pl.Buffered(1))


def _stacked_spec(shape, *lead):
    nd = len(shape) - len(lead)
    return pl.BlockSpec((None,) * len(lead) + tuple(shape[len(lead):]),
                        lambda b, i: tuple(lead) + (0,) * nd, pipeline_mode=pl.Buffered(1))


def _ffn_mixin(x, pos, mod, w_in, w_out, ln_g, ln_b, w_mix, fcs, layer, tt):
    nb, n, _ = x.shape
    has_pos = pos is not None
    tm_spec = pl.BlockSpec((tt, GROUP_W), lambda b, i: (i, b))
    in_specs = [pl.BlockSpec((1, tt, D_MODEL), lambda b, i: (b, i, 0))]
    args = [x]
    if has_pos:
        in_specs.append(pl.BlockSpec((tt, D_MODEL), lambda b, i: (i, 0)))
        args.append(pos)
    in_specs += [
        pl.BlockSpec((1, N_MOD, D_MODEL), lambda b, i: (b, 0, 0)),
        _stacked_spec(w_in.shape, layer, 0), _stacked_spec(w_out.shape, layer, 0),
        _stacked_spec(ln_g.shape, layer), _stacked_spec(ln_b.shape, layer),
        _stacked_spec(w_mix.shape, layer), _const_spec(fcs.shape),
    ]
    args += [mod, w_in, w_out, ln_g, ln_b, w_mix, fcs]
    tm_shape = lambda dt: jax.ShapeDtypeStruct((n, nb * GROUP_W), dt)
    return pl.pallas_call(
        functools.partial(_ffn_mixin_kernel, has_pos, tt),
        grid=(nb, n // tt),
        in_specs=in_specs,
        out_specs=[pl.BlockSpec((1, tt, D_MODEL), lambda b, i: (b, i, 0))] + [tm_spec] * 6,
        out_shape=[jax.ShapeDtypeStruct(x.shape, F32), tm_shape(F32), tm_shape(BF16), tm_shape(BF16),
                   tm_shape(F32), tm_shape(F32), tm_shape(F32)],
        compiler_params=pltpu.CompilerParams(
            dimension_semantics=("parallel", "parallel"), vmem_limit_bytes=VMEM_LIMIT),
        name="ffn_mixin",
    )(*args)


def _interleave(z_ref, slab_ref, nb, tt):
    for b in range(nb):
        for j in range(GROUP_W // LANES):
            lo = b * GROUP_W + j * LANES
            slab_ref[j, pl.ds(b, tt, stride=nb), :] = z_ref[:, lo:lo + LANES]
    return jnp.concatenate([slab_ref[j] for j in range(GROUP_W // LANES)], axis=1)


def _deinterleave(y, slab_ref, o_ref, nb, tt):
    for j in range(GROUP_W // LANES):
        slab_ref[j] = y[:, j * LANES:(j + 1) * LANES]
    for b in range(nb):
        for j in range(GROUP_W // LANES):
            lo = b * GROUP_W + j * LANES
            o_ref[:, lo:lo + LANES] = slab_ref[j, pl.ds(b, tt, stride=nb), :]


def _s5_scan_kernel(nb, tt, zf_ref, zb_ref, bmat_ref, cmat_ref, a_ref, h0_ref,
                    yf_ref, yb_ref, fin_ref, hsf_ref, hsb_ref, st_ref, slab_ref):
    i = pl.program_id(0)

    @pl.when(i == 0)
    def _():
        st_ref[...] = h0_ref[...]

    cw = MXU_DIM
    for d, (z_ref, hs_ref, y_ref) in enumerate(((zf_ref, hsf_ref, yf_ref), (zb_ref, hsb_ref, yb_ref))):
        u = _interleave(z_ref, slab_ref.at[d], nb, tt).astype(BF16)
        y = None
        for c in range(N_STATE // cw):
            re = slice(c * cw, (c + 1) * cw)
            im = slice(N_STATE + c * cw, N_STATE + (c + 1) * cw)
            hs_ref[:, re] = _dot(u, bmat_ref[d, :, re])
            hs_ref[:, im] = _dot(u, bmat_ref[d, :, im])
            ar = jnp.broadcast_to(a_ref[2 * d:2 * d + 1, re], (nb, cw))
            ai = jnp.broadcast_to(a_ref[2 * d + 1:2 * d + 2, re], (nb, cw))
            hr = st_ref[d, :, re]
            hi = st_ref[d, :, im]
            for t in range(tt):
                r0 = (t if d == 0 else tt - 1 - t) * nb
                hr, hi = (ar * hr - ai * hi + hs_ref[r0:r0 + nb, re],
                          ar * hi + ai * hr + hs_ref[r0:r0 + nb, im])
                hs_ref[r0:r0 + nb, re] = hr
                hs_ref[r0:r0 + nb, im] = hi
            st_ref[d, :, re] = hr
            st_ref[d, :, im] = hi
            part = (_dot(hs_ref[:, re].astype(BF16), cmat_ref[d, re, :])
                    + _dot(hs_ref[:, im].astype(BF16), cmat_ref[d, im, :]))
            y = part if y is None else y + part
        _deinterleave(y, slab_ref.at[2 + d], y_ref, nb, tt)
    fin_ref[...] = st_ref[...]


def _s5_scan(za, bmat, cmat, a_vec, h0, nb, n, tt):
    rows = tt * nb
    nt = n // tt
    cs = lambda shape: pl.BlockSpec(shape, lambda i: (0,) * len(shape))
    return pl.pallas_call(
        functools.partial(_s5_scan_kernel, nb, tt),
        grid=(nt,),
        in_specs=[
            pl.BlockSpec((tt, nb * GROUP_W), lambda i: (i, 0)),
            pl.BlockSpec((tt, nb * GROUP_W), lambda i: (nt - 1 - i, 0)),
            cs(bmat.shape), cs(cmat.shape), cs(a_vec.shape), cs(h0.shape),
        ],
        out_specs=[
            pl.BlockSpec((tt, nb * GROUP_W), lambda i: (i, 0)),
            pl.BlockSpec((tt, nb * GROUP_W), lambda i: (nt - 1 - i, 0)),
            cs(h0.shape),
        ],
        out_shape=[jax.ShapeDtypeStruct(za.shape, F32), jax.ShapeDtypeStruct(za.shape, F32),
                   jax.ShapeDtypeStruct(h0.shape, F32)],
        scratch_shapes=[pltpu.VMEM((rows, 2 * N_STATE), F32), pltpu.VMEM((rows, 2 * N_STATE), F32),
                        pltpu.VMEM(h0.shape, F32),
                        pltpu.VMEM((4, GROUP_W // LANES, rows, LANES), F32)],
        compiler_params=pltpu.CompilerParams(
            dimension_semantics=("arbitrary",), vmem_limit_bytes=VMEM_LIMIT),
        name="s5_scan",
    )(za, za, bmat, cmat, a_vec, h0)


def _seq_dft_kernel(c_ref, s_ref, gre_ref, gim_ref, o_ref):
    o_ref[...] = _dot(c_ref[...], gre_ref[...]) + _dot(s_ref[...], gim_ref[...])


def _seq_dft(ctab, stab, gre, gim):
    n, cols = gre.shape
    tl = min(n, SUB_ROWS)
    resident = pl.BlockSpec((n, cols), lambda i: (0, 0), pipeline_mode=pl.Buffered(1))
    return pl.pallas_call(
        _seq_dft_kernel,
        grid=(n // tl,),
        in_specs=[
            pl.BlockSpec((tl, n), lambda i: (i, 0)),
            pl.BlockSpec((tl, n), lambda i: (i, 0)),
            resident, resident,
        ],
        out_specs=pl.BlockSpec((tl, cols), lambda i: (i, 0)),
        out_shape=jax.ShapeDtypeStruct((n, cols), F32),
        compiler_params=pltpu.CompilerParams(
            dimension_semantics=("parallel",), vmem_limit_bytes=VMEM_LIMIT),
        name="seq_dft",
    )(ctab, stab, gre, gim)


def _seq_dft_r4_kernel(n, gre_ref, gim_ref, c_ref, s_ref, twc_ref, tws_ref, o_ref, slab_ref):
    q = n // 4
    xr = [gre_ref[k * q:(k + 1) * q, :].astype(F32) for k in range(4)]
    xi = [gim_ref[k * q:(k + 1) * q, :].astype(F32) for k in range(4)]
    ar, ai = xr[0] + xr[2], xi[0] + xi[2]
    br, bi = xr[1] + xr[3], xi[1] + xi[3]
    cr, ci = xr[0] - xr[2], xi[0] - xi[2]
    dr, di = xr[1] - xr[3], xi[1] - xi[3]
    u = ((ar + br, ai + bi), (cr + di, ci - dr), (ar - br, ai - bi), (cr - di, ci + dr))
    for r, (ur, ui) in enumerate(u):
        if r == 0:
            yr, yi = ur, ui
        else:
            cs = twc_ref[r - 1]
            sn = tws_ref[r - 1]
            yr = ur * cs + ui * sn
            yi = ui * cs - ur * sn
        x = _dot(c_ref[...], yr.astype(BF16)) + _dot(s_ref[...], yi.astype(BF16))
        for j in range(GROUP_W // LANES):
            slab_ref[j, pl.ds(r, q, stride=4), :] = x[:, j * LANES:(j + 1) * LANES]
    for j in range(GROUP_W // LANES):
        o_ref[:, j * LANES:(j + 1) * LANES] = slab_ref[j]


def _seq_dft_r4(ctab, stab, gre, gim):
    n, cols = gre.shape
    q = n // 4
    ang = 2.0 * np.pi * (np.arange(q)[None, :, None] * np.arange(1, 4)[:, None, None]) / n
    twc = jnp.broadcast_to(jnp.asarray(np.cos(ang), F32), (3, q, GROUP_W))
    tws = jnp.broadcast_to(jnp.asarray(np.sin(ang), F32), (3, q, GROUP_W))
    res = lambda shape: pl.BlockSpec(shape, lambda j: (0,) * len(shape), pipeline_mode=pl.Buffered(1))
    col_spec = pl.BlockSpec((n, GROUP_W), lambda j: (0, j))
    return pl.pallas_call(
        functools.partial(_seq_dft_r4_kernel, n),
        grid=(cols // GROUP_W,),
        in_specs=[col_spec, col_spec, res((q, q)), res((q, q)), res(twc.shape), res(tws.shape)],
        out_specs=col_spec,
        out_shape=jax.ShapeDtypeStruct((n, cols), F32),
        scratch_shapes=[pltpu.VMEM((GROUP_W // LANES, n, LANES), F32)],
        compiler_params=pltpu.CompilerParams(
            dimension_semantics=("parallel",), vmem_limit_bytes=VMEM_LIMIT),
        name="seq_dft_r4",
    )(gre, gim, ctab, stab, twc, tws)


def _seg_mean(x, ones_ref):
    hi = x.astype(BF16)
    lo = (x - hi.astype(F32)).astype(BF16)
    return _dot(hi, ones_ref[...]) + _dot(lo, ones_ref[...])


def _group_norm(y, g):
    return y * lax.rsqrt(jnp.mean(y * y, axis=-1, keepdims=True) + LN_EPS) * g


def _mix_ffn_kernel(n, tt, dft_scale,
                    x_ref, mod_ref, za_ref, yf_ref, yb_ref, fr_ref, zc_ref, zcp_ref, zcn_ref, zu_ref, zv_ref,
                    dskip_ref, gluw_ref, glub_ref, fnw_ref, plw_ref, pls_ref, ones_ref, sguw_ref, sgub_ref,
                    mng_ref, wout_ref, w_in_ref, w_out_ref, lng_ref, lnb_ref, o_ref):
    i = pl.program_id(1)
    last = pl.num_programs(1) - 1
    m = lambda k: mod_ref[0, k:k + 1, :]
    lane_grp = lax.broadcasted_iota(jnp.int32, (1, GROUP_W), 1) // POOL_CH
    half = jnp.left_shift(1, lane_grp)
    sub = min(tt, SUB_ROWS)
    prows = sub + 2 * POOL_HALO
    ahead = lambda v, k: pltpu.roll(v, prows - k, axis=0)

    for r0 in range(0, tt, sub):
        rows = slice(r0, r0 + sub)

        ya = za_ref[rows, :] * dskip_ref[...] + yf_ref[rows, :] + yb_ref[rows, :]
        ya = _gelu(ya)
        ya = ya * _sigmoid(_dot(ya.astype(BF16), gluw_ref[...]) + glub_ref[...])

        yb = _dot((fr_ref[rows, :] * dft_scale).astype(BF16), fnw_ref[...])

        zc = zc_ref[rows, :]
        if r0 == 0:
            prev = jnp.where(i > 0, zcp_ref[...], 0.0)
        else:
            prev = zc_ref[r0 - POOL_HALO:r0, :]
        if r0 + sub == tt:
            nxt = jnp.where(i < last, zcn_ref[...], 0.0)
        else:
            nxt = zc_ref[r0 + sub:r0 + sub + POOL_HALO, :]
        padded = jnp.concatenate([prev, zc, nxt], axis=0)
        s2 = padded + ahead(padded, 1)
        s4 = s2 + ahead(s2, 2)
        s8 = s4 + ahead(s4, 4)
        s16 = s8 + ahead(s8, 8)
        win = jnp.where(lane_grp == 0, ahead(s2, 7)[:sub],
                        jnp.where(lane_grp == 1, ahead(s4, 6)[:sub],
                                  jnp.where(lane_grp == 2, ahead(s8, 4)[:sub], s16[:sub])))
        t_glob = i * tt + r0 + lax.broadcasted_iota(jnp.int32, (sub, GROUP_W), 0)
        cnt = jnp.minimum(t_glob + half, n) - jnp.maximum(t_glob - half, 0)
        pooled = win / cnt.astype(F32)
        yc = _dot((pooled - zc).astype(BF16), plw_ref[...]) * pls_ref[...]

        u = _gelu(zu_ref[rows, :])
        v = _gelu(zv_ref[rows, :])
        dv = v - _seg_mean(v, ones_ref)
        vn = dv * lax.rsqrt(_seg_mean(dv * dv, ones_ref) + LN_EPS)
        parts = []
        for c in range(sub // SGU_CHUNK):
            vc = vn[c * SGU_CHUNK:(c + 1) * SGU_CHUNK]
            stacked = jnp.concatenate(
                [jnp.where(lane_grp == g, vc, 0.0).astype(BF16) for g in range(SGU_HEADS)], axis=0)
            parts.append(_dot(sguw_ref[...], stacked) + sgub_ref[...])
        yd = u * jnp.concatenate(parts, axis=0)

        acc = None
        for k, y in enumerate((ya, yb, yc, yd)):
            lo = k * GROUP_W
            yn = _group_norm(y, mng_ref[:, lo:lo + GROUP_W]).astype(BF16)
            o = _dot(yn, wout_ref[lo:lo + GROUP_W, :])
            acc = o if acc is None else acc + o
        x = x_ref[0, rows, :]
        x2 = _layer_norm(ALPHA * x + m(5) * acc, lng_ref[1:2, :], lnb_ref[1:2, :])

        o_ref[0, rows, :] = _ffn_sublayer(x2, m(6), m(7), m(8), w_in_ref, w_out_ref,
                                          lng_ref[2:3, :], lnb_ref[2:3, :])


def _mix_ffn(x, mod, za, yf, yb, fr, zc, zu, zv, consts, tt):
    nb, n, _ = x.shape
    tm_spec = pl.BlockSpec((tt, GROUP_W), lambda b, i: (i, b))
    hb = tt // POOL_HALO
    n_hb = n // POOL_HALO
    prev_spec = pl.BlockSpec((POOL_HALO, GROUP_W), lambda b, i: (jnp.maximum(i * hb - 1, 0), b))
    next_spec = pl.BlockSpec((POOL_HALO, GROUP_W), lambda b, i: (jnp.minimum((i + 1) * hb, n_hb - 1), b))
    x_spec = pl.BlockSpec((1, tt, D_MODEL), lambda b, i: (b, i, 0))
    in_specs = [x_spec, pl.BlockSpec((1, N_MOD, D_MODEL), lambda b, i: (b, 0, 0)),
                tm_spec, tm_spec, tm_spec, tm_spec, tm_spec, prev_spec, next_spec, tm_spec, tm_spec]
    in_specs += [spec for _, spec in consts]
    consts = [arr for arr, _ in consts]
    dft_scale = 1.0 / math.sqrt(n * FNET_CH)
    return pl.pallas_call(
        functools.partial(_mix_ffn_kernel, n, tt, dft_scale),
        grid=(nb, n // tt),
        in_specs=in_specs,
        out_specs=x_spec,
        out_shape=jax.ShapeDtypeStruct(x.shape, F32),
        compiler_params=pltpu.CompilerParams(
            dimension_semantics=("parallel", "parallel"), vmem_limit_bytes=VMEM_LIMIT),
        name="mix_ffn",
    )(x, mod, za, yf, yb, fr, zc, zc, zc, zu, zv, *consts)


def _block_diag(w):
    h, c, d = w.shape
    eye = jnp.eye(h, dtype=w.dtype)
    return jnp.einsum("gcd,gk->gckd", w, eye).reshape(h * c, h * d)


def _channel_dft():
    k = np.arange(FNET_CH)
    ang = 2.0 * np.pi * ((k[:, None] * k[None]) % FNET_CH) / FNET_CH
    eye = np.eye(FNET_HEADS)
    fc = np.kron(eye, np.cos(ang))
    fs = np.kron(eye, np.sin(ang))
    return jnp.asarray(np.concatenate([fc, -fs], axis=1), BF16)


def _s5_params(lam_re, lam_im, log_dt, b_re, b_im, c_re, c_im):
    lam = lax.complex(lam_re.astype(F32), lam_im.astype(F32))
    dt = jnp.exp(log_dt.astype(F32))[..., None]
    a_bar = jnp.exp(lam * dt)
    b_bar = ((a_bar - 1.0) / lam)[..., None] * lax.complex(b_re.astype(F32), b_im.astype(F32))
    eye = jnp.eye(SSM_GROUPS, dtype=F32)
    to_b = lambda w: jnp.einsum("dgph,gk->dghkp", w, eye).reshape(2, GROUP_W, N_STATE)
    bmat = jnp.concatenate([to_b(jnp.real(b_bar)), to_b(jnp.imag(b_bar))], axis=2).astype(BF16)
    to_c = lambda w: jnp.einsum("dghp,gk->dgpkh", w, eye).reshape(2, N_STATE, GROUP_W)
    cmat = jnp.concatenate([to_c(c_re.astype(F32)), -to_c(c_im.astype(F32))], axis=1).astype(BF16)
    a_re = jnp.real(a_bar).reshape(2, N_STATE)
    a_im = jnp.imag(a_bar).reshape(2, N_STATE)
    a_vec = jnp.stack([a_re[0], a_im[0], a_re[1], a_im[1]])
    return bmat, cmat, a_vec


def _sincos_2d(n_tokens):
    rows = n_tokens // GRID_W
    quarter = D_MODEL // 4
    omega = 1.0 / (10000.0 ** (jnp.arange(quarter, dtype=F32) / quarter))
    ang_r = jnp.arange(rows, dtype=F32)[:, None] * omega
    ang_c = jnp.arange(GRID_W, dtype=F32)[:, None] * omega
    emb_r = jnp.concatenate([jnp.sin(ang_r), jnp.cos(ang_r)], -1)
    emb_c = jnp.concatenate([jnp.sin(ang_c), jnp.cos(ang_c)], -1)
    half = D_MODEL // 2
    pos = jnp.concatenate([jnp.broadcast_to(emb_r[:, None], (rows, GRID_W, half)),
                           jnp.broadcast_to(emb_c[None], (rows, GRID_W, half))], -1)
    return pos.reshape(rows * GRID_W, D_MODEL)


def _pack_state(st_re, st_im):
    nb = st_re.shape[0]
    re = jnp.transpose(st_re.reshape(nb, 2, N_STATE), (1, 0, 2))
    im = jnp.transpose(st_im.reshape(nb, 2, N_STATE), (1, 0, 2))
    return jnp.concatenate([re, im], axis=2).astype(F32)


def _unpack_state(fin):
    nb = fin.shape[1]
    re = jnp.transpose(fin[:, :, :N_STATE], (1, 0, 2)).reshape(nb, 2, SSM_GROUPS, SSM_STATE)
    im = jnp.transpose(fin[:, :, N_STATE:], (1, 0, 2)).reshape(nb, 2, SSM_GROUPS, SSM_STATE)
    return re, im


def _tiles(nb, n):
    return min(n, 2 * SUB_ROWS), 512 // nb


def kernel(x_prompt, x_sample, c, state_s5_re, state_s5_im, c_ctx, w_ada, b_ada, ffn_w_in, ffn_w_out,
           w_mix_in, w_mix_out, mix_norm_g, ssm_lam_re, ssm_lam_im, ssm_log_dt, ssm_b_re, ssm_b_im,
           ssm_c_re, ssm_c_im, ssm_d, ssm_glu_w, ssm_glu_b, fnet_w, pool_w, pool_scale, sgu_w, sgu_b,
           ln_g, ln_b):
    nb_p, n_p, _ = x_prompt.shape
    nb_s, n_s, _ = x_sample.shape

    cond = jnp.concatenate([c.astype(F32), c_ctx.astype(F32)[None],
                            jnp.zeros((16 - nb_s - 1, D_MODEL), F32)], axis=0)
    mod_all = _modulation(cond, w_ada.astype(F32), b_ada.astype(F32))
    mod_all = mod_all.reshape(DEPTH, 16, N_MOD, D_MODEL)

    fcs = _channel_dft()
    ones_blk = jnp.asarray(np.kron(np.eye(SGU_HEADS), np.full((SGU_CH, SGU_CH), 1.0 / SGU_CH)), BF16)
    radix4 = {n: n >= RADIX4_MIN_LEN for n in (n_p, n_s)}
    tabs = {n: _dft_tables(n // 4 if radix4[n] else n) for n in sorted({n_p, n_s})}
    pos = _sincos_2d(n_s).astype(F32)

    groups = [
        dict(x=x_prompt.astype(F32), nb=nb_p, n=n_p, zero_state=True),
        dict(x=x_sample.astype(F32), nb=nb_s, n=n_s, zero_state=False),
    ]
    new_re, new_im = [], []
    w_in = ffn_w_in.astype(BF16)
    w_out = ffn_w_out.astype(BF16)
    w_mix = w_mix_in.astype(BF16)
    w_mix_o = w_mix_out.astype(BF16)
    lng = ln_g.astype(F32)
    lnb = ln_b.astype(F32)
    for i in range(DEPTH):
        bmat, cmat, a_vec = _s5_params(ssm_lam_re[i], ssm_lam_im[i], ssm_log_dt[i], ssm_b_re[i],
                                       ssm_b_im[i], ssm_c_re[i], ssm_c_im[i])
        sgu_cat = jnp.transpose(sgu_w[i].astype(F32), (1, 0, 2)).reshape(SGU_CHUNK, SGU_HEADS * SGU_CHUNK)
        sgu_bias = jnp.repeat(jnp.transpose(sgu_b[i].astype(F32)), SGU_CH, axis=1)
        small = [
            ssm_d[i].astype(F32).reshape(1, GROUP_W), ssm_glu_w[i].astype(BF16),
            ssm_glu_b[i].astype(F32).reshape(1, GROUP_W),
            _block_diag(fnet_w[i].astype(F32)).astype(BF16), _block_diag(pool_w[i].astype(F32)).astype(BF16),
            pool_scale[i].astype(F32).reshape(1, GROUP_W), ones_blk, sgu_cat.astype(BF16), sgu_bias,
            mix_norm_g[i].astype(F32).reshape(1, N_MIXERS * GROUP_W),
        ]
        consts = [(a, _const_spec(a.shape)) for a in small] + [
            (w_mix_o, _stacked_spec(w_mix_o.shape, i)),
            (w_in, _stacked_spec(w_in.shape, i, 1)), (w_out, _stacked_spec(w_out.shape, i, 1)),
            (lng, _stacked_spec(lng.shape, i)), (lnb, _stacked_spec(lnb.shape, i)),
        ]
        for grp in groups:
            nb, n = grp["nb"], grp["n"]
            tt, ts = _tiles(nb, n)
            if grp["zero_state"]:
                mod = jnp.broadcast_to(mod_all[i, nb_s][None], (nb, N_MOD, D_MODEL))
                h0 = jnp.zeros((2, nb, 2 * N_STATE), F32)
            else:
                mod = mod_all[i, :nb]
                h0 = _pack_state(state_s5_re[:, i], state_s5_im[:, i])
            use_pos = pos if (i == 0 and not grp["zero_state"]) else None
            x1, za, gre, gim, zc, zu, zv = _ffn_mixin(grp["x"], use_pos, mod, w_in, w_out, lng, lnb,
                                                      w_mix, fcs, i, tt)
            yf, yb, fin = _s5_scan(za, bmat, cmat, a_vec, h0, nb, n, ts)
            ctab, stab = tabs[n]
            fr = (_seq_dft_r4 if radix4[n] else _seq_dft)(ctab, stab, gre, gim)
            grp["x"] = _mix_ffn(x1, mod, za, yf, yb, fr, zc, zu, zv, consts, tt)
            if grp["zero_state"]:
                fre, fim = _unpack_state(fin)
                new_re.append(fre)
                new_im.append(fim)
    return (groups[0]["x"], groups[1]["x"], jnp.stack(new_re, axis=1), jnp.stack(new_im, axis=1))
```

```python
import functools
import math

import numpy as np
import jax
import jax.numpy as jnp
from jax import lax
from jax.experimental import pallas as pl
from jax.experimental.pallas import tpu as pltpu

F32 = jnp.float32
BF16 = jnp.bfloat16

D_MODEL = 1024
DEPTH = 4
GRID_W = 64
GROUP_W = 256
N_MIXERS = 4
SSM_CH = 16
SSM_GROUPS = 16
SSM_STATE = 64
N_STATE = SSM_GROUPS * SSM_STATE
FNET_HEADS = 4
FNET_CH = 64
POOL_CH = 64
SGU_HEADS = 4
SGU_CH = 64
SGU_CHUNK = 128
D_FF = 2816
IN_COLS = 5 * GROUP_W
N_MOD = 9
ALPHA = (2 * DEPTH) ** 0.25
LN_EPS = 1e-5

LANES = 128
MXU_DIM = 256
FF_CHUNKS = (6 * MXU_DIM, 5 * MXU_DIM)
TILE_ROWS = 512
SUB_ROWS = 512
POOL_HALO = 8
DFT_RADIX = 64
RADIX4_MIN_LEN = 1024
VMEM_LIMIT = 56 * 1024 * 1024


def _dot(a, b):
    return jnp.dot(a, b, preferred_element_type=F32)


def _sigmoid(x):
    return 1.0 / (1.0 + jnp.exp(-x))


def _gelu(x):
    k = math.sqrt(2.0 / math.pi)
    h = 0.5 * x
    return h + h * jnp.tanh(x * (k + (k * 0.044715) * (x * x)))


def _layer_norm(r, g, b):
    mu = jnp.mean(r, axis=-1, keepdims=True)
    d = r - mu
    var = jnp.mean(d * d, axis=-1, keepdims=True)
    return d * lax.rsqrt(var + LN_EPS) * g + b


def _swiglu(h, w_in_ref, w_out_ref):
    acc = None
    lo = 0
    for width in FF_CHUNKS:
        a = _dot(h, w_in_ref[:, lo:lo + width])
        g = _dot(h, w_in_ref[:, D_FF + lo:D_FF + lo + width])
        act = (g * _sigmoid(g) * a).astype(BF16)
        o = _dot(act, w_out_ref[lo:lo + width, :])
        acc = o if acc is None else acc + o
        lo += width
    return acc


def _ffn_sublayer(x, shift, scale, gate, w_in_ref, w_out_ref, g, b):
    h = (x * (1.0 + scale) + shift).astype(BF16)
    f = _swiglu(h, w_in_ref, w_out_ref)
    return _layer_norm(ALPHA * x + 0.5 * gate * f, g, b)


def _mod_kernel(c_ref, w_ref, b_ref, o_ref):
    c = c_ref[...]
    s = c * _sigmoid(c)
    rows = s.shape[0]
    s_hi = s.astype(BF16)
    s_lo = (s - s_hi.astype(F32)).astype(BF16)
    w = w_ref[0]
    w_hi = w.astype(BF16)
    w_lo = (w - w_hi.astype(F32)).astype(BF16)
    top = _dot(jnp.concatenate([s_hi, s_lo], axis=0), w_hi)
    o_ref[0] = top[:rows] + top[rows:] + _dot(s_hi, w_lo) + b_ref[0]


def _modulation(cond, w_ada, b_ada):
    rows = cond.shape[0]
    tn = 1536
    n_col = N_MOD * D_MODEL
    return pl.pallas_call(
        _mod_kernel,
        grid=(DEPTH, n_col // tn),
        in_specs=[
            pl.BlockSpec((rows, D_MODEL), lambda l, j: (0, 0)),
            pl.BlockSpec((1, D_MODEL, tn), lambda l, j: (l, 0, j)),
            pl.BlockSpec((1, 1, tn), lambda l, j: (l, 0, j)),
        ],
        out_specs=pl.BlockSpec((1, rows, tn), lambda l, j: (l, 0, j)),
        out_shape=jax.ShapeDtypeStruct((DEPTH, rows, n_col), F32),
        compiler_params=pltpu.CompilerParams(
            dimension_semantics=("parallel", "parallel"), vmem_limit_bytes=VMEM_LIMIT),
        name="adaln_modulation",
    )(cond, w_ada, b_ada.reshape(DEPTH, 1, n_col))


def _dft_table_kernel(ca_ref, sa_ref, cb_ref, sb_ref, c_ref, s_ref):
    ca = ca_ref[0]
    sa = sa_ref[0]
    cb = cb_ref[...]
    sb = sb_ref[...]
    c_ref[...] = (ca * cb - sa * sb).astype(BF16)
    s_ref[...] = (sa * cb + ca * sb).astype(BF16)


def _dft_tables(n):
    r = DFT_RADIX
    k = np.arange(n, dtype=np.int64)
    ang_a = 2.0 * np.pi * ((r * np.arange(n // r, dtype=np.int64)[:, None] * k[None]) % n) / n
    ang_b = 2.0 * np.pi * ((np.arange(r, dtype=np.int64)[:, None] * k[None]) % n) / n
    ca = jnp.asarray(np.cos(ang_a), F32).reshape(n // r, 1, n)
    sa = jnp.asarray(np.sin(ang_a), F32).reshape(n // r, 1, n)
    cb = jnp.asarray(np.cos(ang_b), F32)
    sb = jnp.asarray(np.sin(ang_b), F32)
    return pl.pallas_call(
        _dft_table_kernel,
        grid=(n // r,),
        in_specs=[
            pl.BlockSpec((1, 1, n), lambda i: (i, 0, 0)),
            pl.BlockSpec((1, 1, n), lambda i: (i, 0, 0)),
            pl.BlockSpec((r, n), lambda i: (0, 0)),
            pl.BlockSpec((r, n), lambda i: (0, 0)),
        ],
        out_specs=[pl.BlockSpec((r, n), lambda i: (i, 0)), pl.BlockSpec((r, n), lambda i: (i, 0))],
        out_shape=[jax.ShapeDtypeStruct((n, n), BF16), jax.ShapeDtypeStruct((n, n), BF16)],
        compiler_params=pltpu.CompilerParams(
            dimension_semantics=("parallel",), vmem_limit_bytes=VMEM_LIMIT),
        name="dft_tables",
    )(ca, sa, cb, sb)


def _ffn_mixin_kernel(has_pos, tt, *refs):
    if has_pos:
        x_ref, pos_ref = refs[0], refs[1]
        refs = refs[2:]
    else:
        x_ref, pos_ref = refs[0], None
        refs = refs[1:]
    (mod_ref, w_in_ref, w_out_ref, lng_ref, lnb_ref, wmix_ref, fcs_ref,
     x1_ref, za_ref, gre_ref, gim_ref, zc_ref, zu_ref, zv_ref) = refs
    m = lambda k: mod_ref[0, k:k + 1, :]
    sub = min(tt, SUB_ROWS)
    for r0 in range(0, tt, sub):
        rows = slice(r0, r0 + sub)
        x = x_ref[0, rows, :]
        if pos_ref is not None:
            x = x + pos_ref[rows, :]
        x1 = _ffn_sublayer(x, m(0), m(1), m(2), w_in_ref, w_out_ref, lng_ref[0:1, :], lnb_ref[0:1, :])
        x1_ref[0, rows, :] = x1
        hm = (x1 * (1.0 + m(4)) + m(3)).astype(BF16)
        z = _dot(hm, wmix_ref[...])
        za_ref[rows, :] = z[:, 0:GROUP_W]
        g = _dot(z[:, GROUP_W:2 * GROUP_W].astype(BF16), fcs_ref[...])
        gre_ref[rows, :] = g[:, 0:GROUP_W].astype(BF16)
        gim_ref[rows, :] = g[:, GROUP_W:2 * GROUP_W].astype(BF16)
        zc_ref[rows, :] = z[:, 2 * GROUP_W:3 * GROUP_W]
        zu_ref[rows, :] = z[:, 3 * GROUP_W:4 * GROUP_W]
        zv_ref[rows, :] = z[:, 4 * GROUP_W:5 * GROUP_W]


def _const_spec(shape):
    nd = len(shape)
    return pl.BlockSpec(shape, lambda b, i: (0,) * nd, pipeline_mode=pl.Buffered(1))


def _stacked_spec(shape, *lead):
    nd = len(shape) - len(lead)
    return pl.BlockSpec((None,) * len(lead) + tuple(shape[len(lead):]),
                        lambda b, i: tuple(lead) + (0,) * nd, pipeline_mode=pl.Buffered(1))


def _ffn_mixin(x, pos, mod, w_in, w_out, ln_g, ln_b, w_mix, fcs, layer, tt):
    nb, n, _ = x.shape
    has_pos = pos is not None
    tm_spec = pl.BlockSpec((tt, GROUP_W), lambda b, i: (i, b))
    in_specs = [pl.BlockSpec((1, tt, D_MODEL), lambda b, i: (b, i, 0))]
    args = [x]
    if has_pos:
        in_specs.append(pl.BlockSpec((tt, D_MODEL), lambda b, i: (i, 0)))
        args.append(pos)
    in_specs += [
        pl.BlockSpec((1, N_MOD, D_MODEL), lambda b, i: (b, 0, 0)),
        _stacked_spec(w_in.shape, layer, 0), _stacked_spec(w_out.shape, layer, 0),
        _stacked_spec(ln_g.shape, layer), _stacked_spec(ln_b.shape, layer),
        _stacked_spec(w_mix.shape, layer), _const_spec(fcs.shape),
    ]
    args += [mod, w_in, w_out, ln_g, ln_b, w_mix, fcs]
    tm_shape = lambda dt: jax.ShapeDtypeStruct((n, nb * GROUP_W), dt)
    return pl.pallas_call(
        functools.partial(_ffn_mixin_kernel, has_pos, tt),
        grid=(nb, n // tt),
        in_specs=in_specs,
        out_specs=[pl.BlockSpec((1, tt, D_MODEL), lambda b, i: (b, i, 0))] + [tm_spec] * 6,
        out_shape=[jax.ShapeDtypeStruct(x.shape, F32), tm_shape(F32), tm_shape(BF16), tm_shape(BF16),
                   tm_shape(F32), tm_shape(F32), tm_shape(F32)],
        compiler_params=pltpu.CompilerParams(
            dimension_semantics=("parallel", "parallel"), vmem_limit_bytes=VMEM_LIMIT),
        name="ffn_mixin",
    )(*args)


def _interleave(z_ref, slab_ref, nb, tt):
    for b in range(nb):
        for j in range(GROUP_W // LANES):
            lo = b * GROUP_W + j * LANES
            slab_ref[j, pl.ds(b, tt, stride=nb), :] = z_ref[:, lo:lo + LANES]
    return jnp.concatenate([slab_ref[j] for j in range(GROUP_W // LANES)], axis=1)


def _deinterleave(y, slab_ref, o_ref, nb, tt):
    for j in range(GROUP_W // LANES):
        slab_ref[j] = y[:, j * LANES:(j + 1) * LANES]
    for b in range(nb):
        for j in range(GROUP_W // LANES):
            lo = b * GROUP_W + j * LANES
            o_ref[:, lo:lo + LANES] = slab_ref[j, pl.ds(b, tt, stride=nb), :]


def _s5_scan_kernel(nb, tt, zf_ref, zb_ref, bmat_ref, cmat_ref, a_ref, h0_ref,
                    yf_ref, yb_ref, fin_ref, hsf_ref, hsb_ref, st_ref, slab_ref):
    i = pl.program_id(0)

    @pl.when(i == 0)
    def _():
        st_ref[...] = h0_ref[...]

    cw = MXU_DIM
    for d, (z_ref, hs_ref, y_ref) in enumerate(((zf_ref, hsf_ref, yf_ref), (zb_ref, hsb_ref, yb_ref))):
        u = _interleave(z_ref, slab_ref.at[d], nb, tt).astype(BF16)
        y = None
        for c in range(N_STATE // cw):
            re = slice(c * cw, (c + 1) * cw)
            im = slice(N_STATE + c * cw, N_STATE + (c + 1) * cw)
            hs_ref[:, re] = _dot(u, bmat_ref[d, :, re])
            hs_ref[:, im] = _dot(u, bmat_ref[d, :, im])
            ar = jnp.broadcast_to(a_ref[2 * d:2 * d + 1, re], (nb, cw))
            ai = jnp.broadcast_to(a_ref[2 * d + 1:2 * d + 2, re], (nb, cw))
            hr = st_ref[d, :, re]
            hi = st_ref[d, :, im]
            for t in range(tt):
                r0 = (t if d == 0 else tt - 1 - t) * nb
                hr, hi = (ar * hr - ai * hi + hs_ref[r0:r0 + nb, re],
                          ar * hi + ai * hr + hs_ref[r0:r0 + nb, im])
                hs_ref[r0:r0 + nb, re] = hr
                hs_ref[r0:r0 + nb, im] = hi
            st_ref[d, :, re] = hr
            st_ref[d, :, im] = hi
            part = (_dot(hs_ref[:, re].astype(BF16), cmat_ref[d, re, :])
                    + _dot(hs_ref[:, im].astype(BF16), cmat_ref[d, im, :]))
            y = part if y is None else y + part
        _deinterleave(y, slab_ref.at[2 + d], y_ref, nb, tt)
    fin_ref[...] = st_ref[...]


def _s5_scan(za, bmat, cmat, a_vec, h0, nb, n, tt):
    rows = tt * nb
    nt = n // tt
    cs = lambda shape: pl.BlockSpec(shape, lambda i: (0,) * len(shape))
    return pl.pallas_call(
        functools.partial(_s5_scan_kernel, nb, tt),
        grid=(nt,),
        in_specs=[
            pl.BlockSpec((tt, nb * GROUP_W), lambda i: (i, 0)),
            pl.BlockSpec((tt, nb * GROUP_W), lambda i: (nt - 1 - i, 0)),
            cs(bmat.shape), cs(cmat.shape), cs(a_vec.shape), cs(h0.shape),
        ],
        out_specs=[
            pl.BlockSpec((tt, nb * GROUP_W), lambda i: (i, 0)),
            pl.BlockSpec((tt, nb * GROUP_W), lambda i: (nt - 1 - i, 0)),
            cs(h0.shape),
        ],
        out_shape=[jax.ShapeDtypeStruct(za.shape, F32), jax.ShapeDtypeStruct(za.shape, F32),
                   jax.ShapeDtypeStruct(h0.shape, F32)],
        scratch_shapes=[pltpu.VMEM((rows, 2 * N_STATE), F32), pltpu.VMEM((rows, 2 * N_STATE), F32),
                        pltpu.VMEM(h0.shape, F32),
                        pltpu.VMEM((4, GROUP_W // LANES, rows, LANES), F32)],
        compiler_params=pltpu.CompilerParams(
            dimension_semantics=("arbitrary",), vmem_limit_bytes=VMEM_LIMIT),
        name="s5_scan",
    )(za, za, bmat, cmat, a_vec, h0)


def _seq_dft_kernel(c_ref, s_ref, gre_ref, gim_ref, o_ref):
    o_ref[...] = _dot(c_ref[...], gre_ref[...]) + _dot(s_ref[...], gim_ref[...])


def _seq_dft(ctab, stab, gre, gim):
    n, cols = gre.shape
    tl = min(n, MXU_DIM)
    resident =pl.BlockSpec((n, cols), lambda i: (0, 0), pipeline_mode=pl.Buffered(1))
    return pl.pallas_call(
        _seq_dft_kernel,
        grid=(n // tl,),
        in_specs=[
            pl.BlockSpec((tl, n), lambda i: (i, 0)),
            pl.BlockSpec((tl, n), lambda i: (i, 0)),
            resident, resident,
        ],
        out_specs=pl.BlockSpec((tl, cols), lambda i: (i, 0)),
        out_shape=jax.ShapeDtypeStruct((n, cols), F32),
        compiler_params=pltpu.CompilerParams(
            dimension_semantics=("parallel",), vmem_limit_bytes=VMEM_LIMIT),
        name="seq_dft",
    )(ctab, stab, gre, gim)


def _seq_dft_r4_kernel(n, gre_ref, gim_ref, c_ref, s_ref, twc_ref, tws_ref, o_ref, slab_ref):
    q = n // 4
    xr = [gre_ref[k * q:(k + 1) * q, :].astype(F32) for k in range(4)]
    xi = [gim_ref[k * q:(k + 1) * q, :].astype(F32) for k in range(4)]
    ar, ai = xr[0] + xr[2], xi[0] + xi[2]
    br, bi = xr[1] + xr[3], xi[1] + xi[3]
    cr, ci = xr[0] - xr[2], xi[0] - xi[2]
    dr, di = xr[1] - xr[3], xi[1] - xi[3]
    u = ((ar + br, ai + bi), (cr + di, ci - dr), (ar - br, ai - bi), (cr - di, ci + dr))
    for r, (ur, ui) in enumerate(u):
        if r == 0:
            yr, yi = ur, ui
        else:
            cs = twc_ref[r - 1]
            sn = tws_ref[r - 1]
            yr = ur * cs + ui * sn
            yi = ui * cs - ur * sn
        x = _dot(c_ref[...], yr.astype(BF16)) + _dot(s_ref[...], yi.astype(BF16))
        for j in range(GROUP_W // LANES):
            slab_ref[j, pl.ds(r, q, stride=4), :] = x[:, j * LANES:(j + 1) * LANES]
    for j in range(GROUP_W // LANES):
        o_ref[:, j * LANES:(j + 1) * LANES] = slab_ref[j]


def _seq_dft_r4(ctab, stab, gre, gim):
    n, cols = gre.shape
    q = n // 4
    ang = 2.0 * np.pi * (np.arange(q)[None, :, None] * np.arange(1, 4)[:, None, None]) / n
    twc = jnp.broadcast_to(jnp.asarray(np.cos(ang), F32), (3, q, GROUP_W))
    tws = jnp.broadcast_to(jnp.asarray(np.sin(ang), F32), (3, q, GROUP_W))
    res = lambda shape: pl.BlockSpec(shape, lambda j: (0,) * len(shape), pipeline_mode=pl.Buffered(1))
    col_spec = pl.BlockSpec((n, GROUP_W), lambda j: (0, j))
    return pl.pallas_call(
        functools.partial(_seq_dft_r4_kernel, n),
        grid=(cols // GROUP_W,),
        in_specs=[col_spec, col_spec, res((q, q)), res((q, q)), res(twc.shape), res(tws.shape)],
        out_specs=col_spec,
        out_shape=jax.ShapeDtypeStruct((n, cols), F32),
        scratch_shapes=[pltpu.VMEM((GROUP_W // LANES, n, LANES), F32)],
        compiler_params=pltpu.CompilerParams(
            dimension_semantics=("parallel",), vmem_limit_bytes=VMEM_LIMIT),
        name="seq_dft_r4",
    )(gre, gim, ctab, stab, twc, tws)


def _seg_mean(x, ones_ref):
    hi = x.astype(BF16)
    lo = (x - hi.astype(F32)).astype(BF16)
    return _dot(hi, ones_ref[...]) + _dot(lo, ones_ref[...])


def _group_norm(y, g):
    return y * lax.rsqrt(jnp.mean(y * y, axis=-1, keepdims=True) + LN_EPS) * g


def _mix_ffn_kernel(n, tt, dft_scale,
                    x_ref, mod_ref, za_ref, yf_ref, yb_ref, fr_ref, zc_ref, zcp_ref, zcn_ref, zu_ref, zv_ref,
                    dskip_ref, gluw_ref, glub_ref, fnw_ref, plw_ref, pls_ref, ones_ref, sguw_ref, sgub_ref,
                    mng_ref, wout_ref, w_in_ref, w_out_ref, lng_ref, lnb_ref, o_ref):
    i = pl.program_id(1)
    last = pl.num_programs(1) - 1
    m = lambda k: mod_ref[0, k:k + 1, :]
    lane_grp = lax.broadcasted_iota(jnp.int32, (1, GROUP_W), 1) // POOL_CH
    half = jnp.left_shift(1, lane_grp)
    sub = min(tt, SUB_ROWS)
    prows = sub + 2 * POOL_HALO
    ahead = lambda v, k: pltpu.roll(v, prows - k, axis=0)

    for r0 in range(0, tt, sub):
        rows = slice(r0, r0 + sub)

        ya = za_ref[rows, :] * dskip_ref[...] + yf_ref[rows, :] + yb_ref[rows, :]
        ya = _gelu(ya)
        ya = ya * _sigmoid(_dot(ya.astype(BF16), gluw_ref[...]) + glub_ref[...])

        yb = _dot((fr_ref[rows, :] * dft_scale).astype(BF16), fnw_ref[...])

        zc = zc_ref[rows, :]
        if r0 == 0:
            prev = jnp.where(i > 0, zcp_ref[...], 0.0)
        else:
            prev = zc_ref[r0 - POOL_HALO:r0, :]
        if r0 + sub == tt:
            nxt = jnp.where(i < last, zcn_ref[...], 0.0)
        else:
            nxt = zc_ref[r0 + sub:r0 + sub + POOL_HALO, :]
        padded = jnp.concatenate([prev, zc, nxt], axis=0)
        s2 = padded + ahead(padded, 1)
        s4 = s2 + ahead(s2, 2)
        s8 = s4 + ahead(s4, 4)
        s16 = s8 + ahead(s8, 8)
        win = jnp.where(lane_grp == 0, ahead(s2, 7)[:sub],
                        jnp.where(lane_grp == 1, ahead(s4, 6)[:sub],
                                  jnp.where(lane_grp == 2, ahead(s8, 4)[:sub], s16[:sub])))
        t_glob = i * tt + r0 + lax.broadcasted_iota(jnp.int32, (sub, GROUP_W), 0)
        cnt = jnp.minimum(t_glob + half, n) - jnp.maximum(t_glob - half, 0)
        pooled = win / cnt.astype(F32)
        yc = _dot((pooled - zc).astype(BF16), plw_ref[...]) * pls_ref[...]

        u = _gelu(zu_ref[rows, :])
        v = _gelu(zv_ref[rows, :])
        dv = v - _seg_mean(v, ones_ref)
        vn = dv * lax.rsqrt(_seg_mean(dv * dv, ones_ref) + LN_EPS)
        parts = []
        for c in range(sub // SGU_CHUNK):
            vc = vn[c * SGU_CHUNK:(c + 1) * SGU_CHUNK]
            stacked = jnp.concatenate(
                [jnp.where(lane_grp == g, vc, 0.0).astype(BF16) for g in range(SGU_HEADS)], axis=0)
            parts.append(_dot(sguw_ref[...], stacked) + sgub_ref[...])
        yd = u * jnp.concatenate(parts, axis=0)

        acc = None
        for k, y in enumerate((ya, yb, yc, yd)):
            lo = k * GROUP_W
            yn = _group_norm(y, mng_ref[:, lo:lo + GROUP_W]).astype(BF16)
            o = _dot(yn, wout_ref[lo:lo + GROUP_W, :])
            acc = o if acc is None else acc + o
        x = x_ref[0, rows, :]
        x2 = _layer_norm(ALPHA * x + m(5) * acc, lng_ref[1:2, :], lnb_ref[1:2, :])

        o_ref[0, rows, :] = _ffn_sublayer(x2, m(6), m(7), m(8), w_in_ref, w_out_ref,
                                          lng_ref[2:3, :], lnb_ref[2:3, :])


def _mix_ffn(x, mod, za, yf, yb, fr, zc, zu, zv, consts, tt):
    nb, n, _ = x.shape
    tm_spec = pl.BlockSpec((tt, GROUP_W), lambda b, i: (i, b))
    hb = tt // POOL_HALO
    n_hb = n // POOL_HALO
    prev_spec = pl.BlockSpec((POOL_HALO, GROUP_W), lambda b, i: (jnp.maximum(i * hb - 1, 0), b))
    next_spec = pl.BlockSpec((POOL_HALO, GROUP_W), lambda b, i: (jnp.minimum((i + 1) * hb, n_hb - 1), b))
    x_spec = pl.BlockSpec((1, tt, D_MODEL), lambda b, i: (b, i, 0))
    in_specs = [x_spec, pl.BlockSpec((1, N_MOD, D_MODEL), lambda b, i: (b, 0, 0)),
                tm_spec, tm_spec, tm_spec, tm_spec, tm_spec, prev_spec, next_spec, tm_spec, tm_spec]
    in_specs += [spec for _, spec in consts]
    consts = [arr for arr, _ in consts]
    dft_scale = 1.0 / math.sqrt(n * FNET_CH)
    return pl.pallas_call(
        functools.partial(_mix_ffn_kernel, n, tt, dft_scale),
        grid=(nb, n // tt),
        in_specs=in_specs,
        out_specs=x_spec,
        out_shape=jax.ShapeDtypeStruct(x.shape, F32),
        compiler_params=pltpu.CompilerParams(
            dimension_semantics=("parallel", "parallel"), vmem_limit_bytes=VMEM_LIMIT),
        name="mix_ffn",
    )(x, mod, za, yf, yb, fr, zc, zc, zc, zu, zv, *consts)


def _block_diag(w):
    h, c, d = w.shape
    eye = jnp.eye(h, dtype=w.dtype)
    return jnp.einsum("gcd,gk->gckd", w, eye).reshape(h * c, h * d)


def _channel_dft():
    k = np.arange(FNET_CH)
    ang = 2.0 * np.pi * ((k[:, None] * k[None]) % FNET_CH) / FNET_CH
    eye = np.eye(FNET_HEADS)
    fc = np.kron(eye, np.cos(ang))
    fs = np.kron(eye, np.sin(ang))
    return jnp.asarray(np.concatenate([fc, -fs], axis=1), BF16)


def _s5_params(lam_re, lam_im, log_dt, b_re, b_im, c_re, c_im):
    lam = lax.complex(lam_re.astype(F32), lam_im.astype(F32))
    dt = jnp.exp(log_dt.astype(F32))[..., None]
    a_bar = jnp.exp(lam * dt)
    b_bar = ((a_bar - 1.0) / lam)[..., None] * lax.complex(b_re.astype(F32), b_im.astype(F32))
    eye = jnp.eye(SSM_GROUPS, dtype=F32)
    to_b = lambda w: jnp.einsum("dgph,gk->dghkp", w, eye).reshape(2, GROUP_W, N_STATE)
    bmat = jnp.concatenate([to_b(jnp.real(b_bar)), to_b(jnp.imag(b_bar))], axis=2).astype(BF16)
    to_c = lambda w: jnp.einsum("dghp,gk->dgpkh", w, eye).reshape(2, N_STATE, GROUP_W)
    cmat = jnp.concatenate([to_c(c_re.astype(F32)), -to_c(c_im.astype(F32))], axis=1).astype(BF16)
    a_re = jnp.real(a_bar).reshape(2, N_STATE)
    a_im = jnp.imag(a_bar).reshape(2, N_STATE)
    a_vec = jnp.stack([a_re[0], a_im[0], a_re[1], a_im[1]])
    return bmat, cmat, a_vec


def _sincos_2d(n_tokens):
    rows = n_tokens // GRID_W
    quarter = D_MODEL // 4
    omega = 1.0 / (10000.0 ** (jnp.arange(quarter, dtype=F32) / quarter))
    ang_r = jnp.arange(rows, dtype=F32)[:, None] * omega
    ang_c = jnp.arange(GRID_W, dtype=F32)[:, None] * omega
    emb_r = jnp.concatenate([jnp.sin(ang_r), jnp.cos(ang_r)], -1)
    emb_c = jnp.concatenate([jnp.sin(ang_c), jnp.cos(ang_c)], -1)
    half = D_MODEL // 2
    pos = jnp.concatenate([jnp.broadcast_to(emb_r[:, None], (rows, GRID_W, half)),
                           jnp.broadcast_to(emb_c[None], (rows, GRID_W, half))], -1)
    return pos.reshape(rows * GRID_W, D_MODEL)


def _pack_state(st_re, st_im):
    nb = st_re.shape[0]
    re = jnp.transpose(st_re.reshape(nb, 2, N_STATE), (1, 0, 2))
    im = jnp.transpose(st_im.reshape(nb, 2, N_STATE), (1, 0, 2))
    return jnp.concatenate([re, im], axis=2).astype(F32)


def _unpack_state(fin):
    nb = fin.shape[1]
    re = jnp.transpose(fin[:, :, :N_STATE], (1, 0, 2)).reshape(nb, 2, SSM_GROUPS, SSM_STATE)
    im = jnp.transpose(fin[:, :, N_STATE:], (1, 0, 2)).reshape(nb, 2, SSM_GROUPS, SSM_STATE)
    return re, im


def _tiles(nb, n):
    return min(n, TILE_ROWS), 512 // nb


def kernel(x_prompt, x_sample, c, state_s5_re, state_s5_im, c_ctx, w_ada, b_ada, ffn_w_in, ffn_w_out,
           w_mix_in, w_mix_out, mix_norm_g, ssm_lam_re, ssm_lam_im, ssm_log_dt, ssm_b_re, ssm_b_im,
           ssm_c_re, ssm_c_im, ssm_d, ssm_glu_w, ssm_glu_b, fnet_w, pool_w, pool_scale, sgu_w, sgu_b,
           ln_g, ln_b):
    nb_p, n_p, _ = x_prompt.shape
    nb_s, n_s, _ = x_sample.shape

    cond = jnp.concatenate([c.astype(F32), c_ctx.astype(F32)[None],
                            jnp.zeros((16 - nb_s - 1, D_MODEL), F32)], axis=0)
    mod_all = _modulation(cond, w_ada.astype(F32), b_ada.astype(F32))
    mod_all = mod_all.reshape(DEPTH, 16, N_MOD, D_MODEL)

    fcs = _channel_dft()
    ones_blk = jnp.asarray(np.kron(np.eye(SGU_HEADS), np.full((SGU_CH, SGU_CH), 1.0 / SGU_CH)), BF16)
    radix4 = {n: n >= RADIX4_MIN_LEN for n in (n_p, n_s)}
    tabs = {n: _dft_tables(n // 4 if radix4[n] else n) for n in sorted({n_p, n_s})}
    pos = _sincos_2d(n_s).astype(F32)

    groups = [
        dict(x=x_prompt.astype(F32), nb=nb_p, n=n_p, zero_state=True),
        dict(x=x_sample.astype(F32), nb=nb_s, n=n_s, zero_state=False),
    ]
    new_re, new_im = [], []
    w_in = ffn_w_in.astype(BF16)
    w_out = ffn_w_out.astype(BF16)
    w_mix = w_mix_in.astype(BF16)
    w_mix_o = w_mix_out.astype(BF16)
    lng = ln_g.astype(F32)
    lnb = ln_b.astype(F32)
    for i in range(DEPTH):
        bmat, cmat, a_vec = _s5_params(ssm_lam_re[i], ssm_lam_im[i], ssm_log_dt[i], ssm_b_re[i],
                                       ssm_b_im[i], ssm_c_re[i], ssm_c_im[i])
        sgu_cat = jnp.transpose(sgu_w[i].astype(F32), (1, 0, 2)).reshape(SGU_CHUNK, SGU_HEADS * SGU_CHUNK)
        sgu_bias = jnp.repeat(jnp.transpose(sgu_b[i].astype(F32)), SGU_CH, axis=1)
        small = [
            ssm_d[i].astype(F32).reshape(1, GROUP_W), ssm_glu_w[i].astype(BF16),
            ssm_glu_b[i].astype(F32).reshape(1, GROUP_W),
            _block_diag(fnet_w[i].astype(F32)).astype(BF16), _block_diag(pool_w[i].astype(F32)).astype(BF16),
            pool_scale[i].astype(F32).reshape(1, GROUP_W), ones_blk, sgu_cat.astype(BF16), sgu_bias,
            mix_norm_g[i].astype(F32).reshape(1, N_MIXERS * GROUP_W),
        ]
        consts = [(a, _const_spec(a.shape)) for a in small] + [
            (w_mix_o, _stacked_spec(w_mix_o.shape, i)),
            (w_in, _stacked_spec(w_in.shape, i, 1)), (w_out, _stacked_spec(w_out.shape, i, 1)),
            (lng, _stacked_spec(lng.shape, i)), (lnb, _stacked_spec(lnb.shape, i)),
        ]
        for grp in groups:
            nb, n = grp["nb"], grp["n"]
            tt, ts = _tiles(nb, n)
            if grp["zero_state"]:
                mod = jnp.broadcast_to(mod_all[i, nb_s][None], (nb, N_MOD, D_MODEL))
                h0 = jnp.zeros((2, nb, 2 * N_STATE), F32)
            else:
                mod = mod_all[i, :nb]
                h0 = _pack_state(state_s5_re[:, i], state_s5_im[:, i])
            use_pos = pos if (i == 0 and not grp["zero_state"]) else None
            x1, za, gre, gim, zc, zu, zv = _ffn_mixin(grp["x"], use_pos, mod, w_in, w_out, lng, lnb,
                                                      w_mix, fcs, i, tt)
            yf, yb, fin = _s5_scan(za, bmat, cmat, a_vec, h0, nb, n, ts)
            ctab, stab = tabs[n]
            fr = (_seq_dft_r4 if radix4[n] else _seq_dft)(ctab, stab, gre, gim)
            grp["x"] = _mix_ffn(x1, mod, za, yf, yb, fr, zc, zu, zv, consts, tt)
            if grp["zero_state"]:
                fre, fim = _unpack_state(fin)
                new_re.append(fre)
                new_im.append(fim)
    return (groups[0]["x"], groups[1]["x"], jnp.stack(new_re, axis=1), jnp.stack(new_im, axis=1))
```

```python
import functools
import math

import numpy as np
import jax
import jax.numpy as jnp
from jax import lax
from jax.experimental import pallas as pl
from jax.experimental.pallas import tpu as pltpu

F32 = jnp.float32
BF16 = jnp.bfloat16

D_MODEL = 1024
DEPTH = 4
GRID_W = 64
GROUP_W = 256
N_MIXERS = 4
SSM_CH = 16
SSM_GROUPS = 16
SSM_STATE = 64
N_STATE = SSM_GROUPS * SSM_STATE
FNET_HEADS = 4
FNET_CH = 64
POOL_CH = 64
SGU_HEADS = 4
SGU_CH = 64
SGU_CHUNK = 128
D_FF = 2816
IN_COLS = 5 * GROUP_W
N_MOD = 9
ALPHA = (2 * DEPTH) ** 0.25
LN_EPS = 1e-5

LANES = 128
MXU_DIM = 256
FF_CHUNKS = (6 * MXU_DIM, 5 * MXU_DIM)
TILE_ROWS = 512
SUB_ROWS = 512
SCAN_ROWS = 1024
POOL_HALO = 8
DFT_RADIX = 64
RADIX4_MIN_LEN = 1024
VMEM_LIMIT = 56 * 1024 * 1024


def _dot(a, b):
    return jnp.dot(a, b, preferred_element_type=F32)


def _sigmoid(x):
    return 1.0 / (1.0 + jnp.exp(-x))


def _gelu(x):
    k = math.sqrt(2.0 / math.pi)
    h = 0.5 * x
    return h + h * jnp.tanh(x * (k + (k * 0.044715) * (x * x)))


def _layer_norm(r, g, b):
    mu = jnp.mean(r, axis=-1, keepdims=True)
    d = r - mu
    var = jnp.mean(d * d, axis=-1, keepdims=True)
    return d * lax.rsqrt(var + LN_EPS) * g + b


def _swiglu(h, w_in_ref, w_out_ref):
    acc = None
    lo = 0
    for width in FF_CHUNKS:
        a = _dot(h, w_in_ref[:, lo:lo + width])
        g = _dot(h, w_in_ref[:, D_FF + lo:D_FF + lo + width])
        act = (g * _sigmoid(g) * a).astype(BF16)
        o = _dot(act, w_out_ref[lo:lo + width, :])
        acc = o if acc is None else acc + o
        lo += width
    return acc


def _ffn_sublayer(x, shift, scale, gate, w_in_ref, w_out_ref, g, b):
    h = (x * (1.0 + scale) + shift).astype(BF16)
    f = _swiglu(h, w_in_ref, w_out_ref)
    return _layer_norm(ALPHA * x + 0.5 * gate * f, g, b)


def _mod_kernel(c_ref, w_ref, b_ref, o_ref):
    c = c_ref[...]
    s = c * _sigmoid(c)
    rows = s.shape[0]
    s_hi = s.astype(BF16)
    s_lo = (s - s_hi.astype(F32)).astype(BF16)
    w = w_ref[0]
    w_hi = w.astype(BF16)
    w_lo = (w - w_hi.astype(F32)).astype(BF16)
    top = _dot(jnp.concatenate([s_hi, s_lo], axis=0), w_hi)
    o_ref[0] = top[:rows] + top[rows:] + _dot(s_hi, w_lo) + b_ref[0]


def _modulation(cond, w_ada, b_ada):
    rows = cond.shape[0]
    tn = 1536
    n_col = N_MOD * D_MODEL
    return pl.pallas_call(
        _mod_kernel,
        grid=(DEPTH, n_col // tn),
        in_specs=[
            pl.BlockSpec((rows, D_MODEL), lambda l, j: (0, 0)),
            pl.BlockSpec((1, D_MODEL, tn), lambda l, j: (l, 0, j)),
            pl.BlockSpec((1, 1, tn), lambda l, j: (l, 0, j)),
        ],
        out_specs=pl.BlockSpec((1, rows, tn), lambda l, j: (l, 0, j)),
        out_shape=jax.ShapeDtypeStruct((DEPTH, rows, n_col), F32),
        compiler_params=pltpu.CompilerParams(
            dimension_semantics=("parallel", "parallel"), vmem_limit_bytes=VMEM_LIMIT),
        name="adaln_modulation",
    )(cond, w_ada, b_ada.reshape(DEPTH, 1, n_col))


def _dft_table_kernel(ca_ref, sa_ref, cb_ref, sb_ref, c_ref, s_ref):
    ca = ca_ref[0]
    sa = sa_ref[0]
    cb = cb_ref[...]
    sb = sb_ref[...]
    c_ref[...] = (ca * cb - sa * sb).astype(BF16)
    s_ref[...] = (sa * cb + ca * sb).astype(BF16)


def _dft_tables(n):
    r = DFT_RADIX
    k = np.arange(n, dtype=np.int64)
    ang_a = 2.0 * np.pi * ((r * np.arange(n // r, dtype=np.int64)[:, None] * k[None]) % n) / n
    ang_b = 2.0 * np.pi * ((np.arange(r, dtype=np.int64)[:, None] * k[None]) % n) / n
    ca = jnp.asarray(np.cos(ang_a), F32).reshape(n // r, 1, n)
    sa = jnp.asarray(np.sin(ang_a), F32).reshape(n // r, 1, n)
    cb = jnp.asarray(np.cos(ang_b), F32)
    sb = jnp.asarray(np.sin(ang_b), F32)
    return pl.pallas_call(
        _dft_table_kernel,
        grid=(n // r,),
        in_specs=[
            pl.BlockSpec((1, 1, n), lambda i: (i, 0, 0)),
            pl.BlockSpec((1, 1, n), lambda i: (i, 0, 0)),
            pl.BlockSpec((r, n), lambda i: (0, 0)),
            pl.BlockSpec((r, n), lambda i: (0, 0)),
        ],
        out_specs=[pl.BlockSpec((r, n), lambda i: (i, 0)), pl.BlockSpec((r, n), lambda i: (i, 0))],
        out_shape=[jax.ShapeDtypeStruct((n, n), BF16), jax.ShapeDtypeStruct((n, n), BF16)],
        compiler_params=pltpu.CompilerParams(
            dimension_semantics=("parallel",), vmem_limit_bytes=VMEM_LIMIT),
        name="dft_tables",
    )(ca, sa, cb, sb)


def _ffn_mixin_kernel(has_pos, tt, *refs):
    if has_pos:
        x_ref, pos_ref = refs[0], refs[1]
        refs = refs[2:]
    else:
        x_ref, pos_ref = refs[0], None
        refs = refs[1:]
    (mod_ref, w_in_ref, w_out_ref, lng_ref, lnb_ref, wmix_ref, fcs_ref,
     x1_ref, za_ref, gre_ref, gim_ref, zc_ref, zu_ref, zv_ref) = refs
    m = lambda k: mod_ref[0, k:k + 1, :]
    sub = min(tt, SUB_ROWS)
    for r0 in range(0, tt, sub):
        rows = slice(r0, r0 + sub)
        x = x_ref[0, rows, :]
        if pos_ref is not None:
            x = x + pos_ref[rows, :]
        x1 = _ffn_sublayer(x, m(0), m(1), m(2), w_in_ref, w_out_ref, lng_ref[0:1, :], lnb_ref[0:1, :])
        x1_ref[0, rows, :] = x1
        hm = (x1 * (1.0 + m(4)) + m(3)).astype(BF16)
        z = _dot(hm, wmix_ref[...])
        za_ref[rows, :] = z[:, 0:GROUP_W]
        g = _dot(z[:, GROUP_W:2 * GROUP_W].astype(BF16), fcs_ref[...])
        gre_ref[rows, :] = g[:, 0:GROUP_W].astype(BF16)
        gim_ref[rows, :] = g[:, GROUP_W:2 * GROUP_W].astype(BF16)
        zc_ref[rows, :] = z[:, 2 * GROUP_W:3 * GROUP_W]
        zu_ref[rows, :] = z[:, 3 * GROUP_W:4 * GROUP_W]
        zv_ref[rows, :] = z[:, 4 * GROUP_W:5 * GROUP_W]


def _const_spec(shape):
    nd = len(shape)
    return pl.BlockSpec(shape, lambda b, i: (0,) * nd, pipeline_mode=pl.Buffered(1))


def _stacked_spec(shape, *lead):
    nd = len(shape) - len(lead)
    return pl.BlockSpec((None,) * len(lead) + tuple(shape[len(lead):]),
                        lambda b, i: tuple(lead) + (0,) * nd, pipeline_mode=pl.Buffered(1))


def _ffn_mixin(x, pos, mod, w_in, w_out, ln_g, ln_b, w_mix, fcs, layer, tt):
    nb, n, _ = x.shape
    has_pos = pos is not None
    tm_spec = pl.BlockSpec((tt, GROUP_W), lambda b, i: (i, b))
    in_specs = [pl.BlockSpec((1, tt, D_MODEL), lambda b, i: (b, i, 0))]
    args = [x]
    if has_pos:
        in_specs.append(pl.BlockSpec((tt, D_MODEL), lambda b, i: (i, 0)))
        args.append(pos)
    in_specs += [
        pl.BlockSpec((1, N_MOD, D_MODEL), lambda b, i: (b, 0, 0)),
        _stacked_spec(w_in.shape, layer, 0), _stacked_spec(w_out.shape, layer, 0),
        _stacked_spec(ln_g.shape, layer), _stacked_spec(ln_b.shape, layer),
        _stacked_spec(w_mix.shape, layer), _const_spec(fcs.shape),
    ]
    args += [mod, w_in, w_out, ln_g, ln_b, w_mix, fcs]
    tm_shape = lambda dt: jax.ShapeDtypeStruct((n, nb * GROUP_W), dt)
    return pl.pallas_call(
        functools.partial(_ffn_mixin_kernel, has_pos, tt),
        grid=(nb, n // tt),
        in_specs=in_specs,
        out_specs=[pl.BlockSpec((1, tt, D_MODEL), lambda b, i: (b, i, 0))] + [tm_spec] * 6,
        out_shape=[jax.ShapeDtypeStruct(x.shape, F32), tm_shape(F32), tm_shape(BF16), tm_shape(BF16),
                   tm_shape(F32), tm_shape(F32), tm_shape(F32)],
        compiler_params=pltpu.CompilerParams(
            dimension_semantics=("parallel", "parallel"), vmem_limit_bytes=VMEM_LIMIT),
        name="ffn_mixin",
    )(*args)


def _interleave(z_ref, slab_ref, nb, tt):
    for b in range(nb):
        for j in range(GROUP_W // LANES):
            lo = b * GROUP_W + j * LANES
            slab_ref[j, pl.ds(b, tt, stride=nb), :] = z_ref[:, lo:lo + LANES]
    return jnp.concatenate([slab_ref[j] for j in range(GROUP_W // LANES)], axis=1)


def _deinterleave(y, slab_ref, o_ref, nb, tt):
    for j in range(GROUP_W // LANES):
        slab_ref[j] = y[:, j * LANES:(j + 1) * LANES]
    for b in range(nb):
        for j in range(GROUP_W // LANES):
            lo = b * GROUP_W + j * LANES
            o_ref[:, lo:lo + LANES] = slab_ref[j, pl.ds(b, tt, stride=nb), :]


def _s5_scan_kernel(nb, tt, zf_ref, zb_ref, bmat_ref, cmat_ref, a_ref, h0_ref,
                    yf_ref, yb_ref, fin_ref, hsf_ref, hsb_ref, st_ref, slab_ref):
    i = pl.program_id(0)

    @pl.when(i == 0)
    def _():
        st_ref[...] = h0_ref[...]

    cw = MXU_DIM
    for d, (z_ref, hs_ref, y_ref) in enumerate(((zf_ref, hsf_ref, yf_ref), (zb_ref, hsb_ref, yb_ref))):
        u = _interleave(z_ref, slab_ref.at[d], nb, tt).astype(BF16)
        y = None
        for c in range(N_STATE // cw):
            re = slice(c * cw, (c + 1) * cw)
            im = slice(N_STATE + c * cw, N_STATE + (c + 1) * cw)
            hs_ref[:, re] = _dot(u, bmat_ref[d, :, re])
            hs_ref[:, im] = _dot(u, bmat_ref[d, :, im])
            ar = jnp.broadcast_to(a_ref[2 * d:2 * d + 1, re], (nb, cw))
            ai = jnp.broadcast_to(a_ref[2 * d + 1:2 * d + 2, re], (nb, cw))
            hr = st_ref[d, :, re]
            hi = st_ref[d, :, im]
            for t in range(tt):
                r0 = (t if d == 0 else tt - 1 - t) * nb
                hr, hi = (ar * hr - ai * hi + hs_ref[r0:r0 + nb, re],
                          ar * hi + ai * hr + hs_ref[r0:r0 + nb, im])
                hs_ref[r0:r0 + nb, re] = hr
                hs_ref[r0:r0 + nb, im] = hi
            st_ref[d, :, re] = hr
            st_ref[d, :, im] = hi
            part = (_dot(hs_ref[:, re].astype(BF16), cmat_ref[d, re, :])
                    + _dot(hs_ref[:, im].astype(BF16), cmat_ref[d, im, :]))
            y = part if y is None else y + part
        _deinterleave(y, slab_ref.at[2 + d], y_ref, nb, tt)
    fin_ref[...] = st_ref[...]


def _s5_scan(za, bmat, cmat, a_vec, h0, nb, n, tt):
    rows = tt * nb
    nt = n // tt
    cs = lambda shape: pl.BlockSpec(shape, lambda i: (0,) * len(shape))
    return pl.pallas_call(
        functools.partial(_s5_scan_kernel, nb, tt),
        grid=(nt,),
        in_specs=[
            pl.BlockSpec((tt, nb * GROUP_W), lambda i: (i, 0)),
            pl.BlockSpec((tt, nb * GROUP_W), lambda i: (nt - 1 - i, 0)),
            cs(bmat.shape), cs(cmat.shape), cs(a_vec.shape), cs(h0.shape),
        ],
        out_specs=[
            pl.BlockSpec((tt, nb * GROUP_W), lambda i: (i, 0)),
            pl.BlockSpec((tt, nb * GROUP_W), lambda i: (nt - 1 - i, 0)),
            cs(h0.shape),
        ],
        out_shape=[jax.ShapeDtypeStruct(za.shape, F32), jax.ShapeDtypeStruct(za.shape, F32),
                   jax.ShapeDtypeStruct(h0.shape, F32)],
        scratch_shapes=[pltpu.VMEM((rows, 2 * N_STATE), F32), pltpu.VMEM((rows, 2 * N_STATE), F32),
                        pltpu.VMEM(h0.shape, F32),
                        pltpu.VMEM((4, GROUP_W // LANES, rows, LANES), F32)],
        compiler_params=pltpu.CompilerParams(
            dimension_semantics=("arbitrary",), vmem_limit_bytes=VMEM_LIMIT),
        name="s5_scan",
    )(za, za, bmat, cmat, a_vec, h0)


def _seq_dft_kernel(c_ref, s_ref, gre_ref, gim_ref, o_ref):
    o_ref[...] = _dot(c_ref[...], gre_ref[...]) + _dot(s_ref[...], gim_ref[...])


def _seq_dft(ctab, stab, gre, gim):
    n, cols = gre.shape
    tl = min(n, MXU_DIM)
    resident =pl.BlockSpec((n, cols), lambda i: (0, 0), pipeline_mode=pl.Buffered(1))
    return pl.pallas_call(
        _seq_dft_kernel,
        grid=(n // tl,),
        in_specs=[
            pl.BlockSpec((tl, n), lambda i: (i, 0)),
            pl.BlockSpec((tl, n), lambda i: (i, 0)),
            resident, resident,
        ],
        out_specs=pl.BlockSpec((tl, cols), lambda i: (i, 0)),
        out_shape=jax.ShapeDtypeStruct((n, cols), F32),
        compiler_params=pltpu.CompilerParams(
            dimension_semantics=("parallel",), vmem_limit_bytes=VMEM_LIMIT),
        name="seq_dft",
    )(ctab, stab, gre, gim)


def _seq_dft_r4_kernel(n, gre_ref, gim_ref, c_ref, s_ref, twc_ref, tws_ref, o_ref, slab_ref):
    q = n // 4
    xr = [gre_ref[k * q:(k + 1) * q, :].astype(F32) for k in range(4)]
    xi = [gim_ref[k * q:(k + 1) * q, :].astype(F32) for k in range(4)]
    ar, ai = xr[0] + xr[2], xi[0] + xi[2]
    br, bi = xr[1] + xr[3], xi[1] + xi[3]
    cr, ci = xr[0] - xr[2], xi[0] - xi[2]
    dr, di = xr[1] - xr[3], xi[1] - xi[3]
    u = ((ar + br, ai + bi), (cr + di, ci - dr), (ar - br, ai - bi), (cr - di, ci + dr))
    for r, (ur, ui) in enumerate(u):
        if r == 0:
            yr, yi = ur, ui
        else:
            cs = twc_ref[r - 1]
            sn = tws_ref[r - 1]
            yr = ur * cs + ui * sn
            yi = ui * cs - ur * sn
        x = _dot(c_ref[...], yr.astype(BF16)) + _dot(s_ref[...], yi.astype(BF16))
        for j in range(GROUP_W // LANES):
            slab_ref[j, pl.ds(r, q, stride=4), :] = x[:, j * LANES:(j + 1) * LANES]
    for j in range(GROUP_W // LANES):
        o_ref[:, j * LANES:(j + 1) * LANES] = slab_ref[j]


def _seq_dft_r4(ctab, stab, gre, gim):
    n, cols = gre.shape
    q = n // 4
    ang = 2.0 * np.pi * (np.arange(q)[None, :, None] * np.arange(1, 4)[:, None, None]) / n
    twc = jnp.broadcast_to(jnp.asarray(np.cos(ang), F32), (3, q, GROUP_W))
    tws = jnp.broadcast_to(jnp.asarray(np.sin(ang), F32), (3, q, GROUP_W))
    res = lambda shape: pl.BlockSpec(shape, lambda j: (0,) * len(shape), pipeline_mode=pl.Buffered(1))
    col_spec = pl.BlockSpec((n, GROUP_W), lambda j: (0, j))
    return pl.pallas_call(
        functools.partial(_seq_dft_r4_kernel, n),
        grid=(cols // GROUP_W,),
        in_specs=[col_spec, col_spec, res((q, q)), res((q, q)), res(twc.shape), res(tws.shape)],
        out_specs=col_spec,
        out_shape=jax.ShapeDtypeStruct((n, cols), F32),
        scratch_shapes=[pltpu.VMEM((GROUP_W // LANES, n, LANES), F32)],
        compiler_params=pltpu.CompilerParams(
            dimension_semantics=("parallel",), vmem_limit_bytes=VMEM_LIMIT),
        name="seq_dft_r4",
    )(gre, gim, ctab, stab, twc, tws)


def _seg_mean(x, ones_ref):
    hi = x.astype(BF16)
    lo = (x - hi.astype(F32)).astype(BF16)
    return _dot(hi, ones_ref[...]) + _dot(lo, ones_ref[...])


def _group_norm(y, g):
    return y * lax.rsqrt(jnp.mean(y * y, axis=-1, keepdims=True) + LN_EPS) * g


def _mix_ffn_kernel(n, tt, dft_scale,
                    x_ref, mod_ref, za_ref, yf_ref, yb_ref, fr_ref, zc_ref, zcp_ref, zcn_ref, zu_ref, zv_ref,
                    dskip_ref, gluw_ref, glub_ref, fnw_ref, plw_ref, pls_ref, ones_ref, sguw_ref, sgub_ref,
                    mng_ref, wout_ref, w_in_ref, w_out_ref, lng_ref, lnb_ref, o_ref):
    i = pl.program_id(1)
    last = pl.num_programs(1) - 1
    m = lambda k: mod_ref[0, k:k + 1, :]
    lane_grp = lax.broadcasted_iota(jnp.int32, (1, GROUP_W), 1) // POOL_CH
    half = jnp.left_shift(1, lane_grp)
    sub = min(tt, SUB_ROWS)
    prows = sub + 2 * POOL_HALO
    ahead = lambda v, k: pltpu.roll(v, prows - k, axis=0)

    for r0 in range(0, tt, sub):
        rows = slice(r0, r0 + sub)

        ya = za_ref[rows, :] * dskip_ref[...] + yf_ref[rows, :] + yb_ref[rows, :]
        ya = _gelu(ya)
        ya = ya * _sigmoid(_dot(ya.astype(BF16), gluw_ref[...]) + glub_ref[...])

        yb = _dot((fr_ref[rows, :] * dft_scale).astype(BF16), fnw_ref[...])

        zc = zc_ref[rows, :]
        if r0 == 0:
            prev = jnp.where(i > 0, zcp_ref[...], 0.0)
        else:
            prev = zc_ref[r0 - POOL_HALO:r0, :]
        if r0 + sub == tt:
            nxt = jnp.where(i < last, zcn_ref[...], 0.0)
        else:
            nxt = zc_ref[r0 + sub:r0 + sub + POOL_HALO, :]
        padded = jnp.concatenate([prev, zc, nxt], axis=0)
        s2 = padded + ahead(padded, 1)
        s4 = s2 + ahead(s2, 2)
        s8 = s4 + ahead(s4, 4)
        s16 = s8 + ahead(s8, 8)
        win = jnp.where(lane_grp == 0, ahead(s2, 7)[:sub],
                        jnp.where(lane_grp == 1, ahead(s4, 6)[:sub],
                                  jnp.where(lane_grp == 2, ahead(s8, 4)[:sub], s16[:sub])))
        t_glob = i * tt + r0 + lax.broadcasted_iota(jnp.int32, (sub, GROUP_W), 0)
        cnt = jnp.minimum(t_glob + half, n) - jnp.maximum(t_glob - half, 0)
        pooled = win / cnt.astype(F32)
        yc = _dot((pooled - zc).astype(BF16), plw_ref[...]) * pls_ref[...]

        u = _gelu(zu_ref[rows, :])
        v = _gelu(zv_ref[rows, :])
        dv = v - _seg_mean(v, ones_ref)
        vn = dv * lax.rsqrt(_seg_mean(dv * dv, ones_ref) + LN_EPS)
        parts = []
        for c in range(sub // SGU_CHUNK):
            vc = vn[c * SGU_CHUNK:(c + 1) * SGU_CHUNK]
            stacked = jnp.concatenate(
                [jnp.where(lane_grp == g, vc, 0.0).astype(BF16) for g in range(SGU_HEADS)], axis=0)
            parts.append(_dot(sguw_ref[...], stacked) + sgub_ref[...])
        yd = u * jnp.concatenate(parts, axis=0)

        acc = None
        for k, y in enumerate((ya, yb, yc, yd)):
            lo = k * GROUP_W
            yn = _group_norm(y, mng_ref[:, lo:lo + GROUP_W]).astype(BF16)
            o = _dot(yn, wout_ref[lo:lo + GROUP_W, :])
            acc = o if acc is None else acc + o
        x = x_ref[0, rows, :]
        x2 = _layer_norm(ALPHA * x + m(5) * acc, lng_ref[1:2, :], lnb_ref[1:2, :])

        o_ref[0, rows, :] = _ffn_sublayer(x2, m(6), m(7), m(8), w_in_ref, w_out_ref,
                                          lng_ref[2:3, :], lnb_ref[2:3, :])


def _mix_ffn(x, mod, za, yf, yb, fr, zc, zu, zv, consts, tt):
    nb, n, _ = x.shape
    tm_spec = pl.BlockSpec((tt, GROUP_W), lambda b, i: (i, b))
    hb = tt // POOL_HALO
    n_hb = n // POOL_HALO
    prev_spec = pl.BlockSpec((POOL_HALO, GROUP_W), lambda b, i: (jnp.maximum(i * hb - 1, 0), b))
    next_spec = pl.BlockSpec((POOL_HALO, GROUP_W), lambda b, i: (jnp.minimum((i + 1) * hb, n_hb - 1), b))
    x_spec = pl.BlockSpec((1, tt, D_MODEL), lambda b, i: (b, i, 0))
    in_specs = [x_spec, pl.BlockSpec((1, N_MOD, D_MODEL), lambda b, i: (b, 0, 0)),
                tm_spec, tm_spec, tm_spec, tm_spec, tm_spec, prev_spec, next_spec, tm_spec, tm_spec]
    in_specs += [spec for _, spec in consts]
    consts = [arr for arr, _ in consts]
    dft_scale = 1.0 / math.sqrt(n * FNET_CH)
    return pl.pallas_call(
        functools.partial(_mix_ffn_kernel, n, tt, dft_scale),
        grid=(nb, n // tt),
        in_specs=in_specs,
        out_specs=x_spec,
        out_shape=jax.ShapeDtypeStruct(x.shape, F32),
        compiler_params=pltpu.CompilerParams(
            dimension_semantics=("parallel", "parallel"), vmem_limit_bytes=VMEM_LIMIT),
        name="mix_ffn",
    )(x, mod, za, yf, yb, fr, zc, zc, zc, zu, zv, *consts)


def _block_diag(w):
    h, c, d = w.shape
    eye = jnp.eye(h, dtype=w.dtype)
    return jnp.einsum("gcd,gk->gckd", w, eye).reshape(h * c, h * d)


def _channel_dft():
    k = np.arange(FNET_CH)
    ang = 2.0 * np.pi * ((k[:, None] * k[None]) % FNET_CH) / FNET_CH
    eye = np.eye(FNET_HEADS)
    fc = np.kron(eye, np.cos(ang))
    fs = np.kron(eye, np.sin(ang))
    return jnp.asarray(np.concatenate([fc, -fs], axis=1), BF16)


def _s5_params(lam_re, lam_im, log_dt, b_re, b_im, c_re, c_im):
    lam = lax.complex(lam_re.astype(F32), lam_im.astype(F32))
    dt = jnp.exp(log_dt.astype(F32))[..., None]
    a_bar = jnp.exp(lam * dt)
    b_bar = ((a_bar - 1.0) / lam)[..., None] * lax.complex(b_re.astype(F32), b_im.astype(F32))
    eye = jnp.eye(SSM_GROUPS, dtype=F32)
    to_b = lambda w: jnp.einsum("dgph,gk->dghkp", w, eye).reshape(2, GROUP_W, N_STATE)
    bmat = jnp.concatenate([to_b(jnp.real(b_bar)), to_b(jnp.imag(b_bar))], axis=2).astype(BF16)
    to_c = lambda w: jnp.einsum("dghp,gk->dgpkh", w, eye).reshape(2, N_STATE, GROUP_W)
    cmat = jnp.concatenate([to_c(c_re.astype(F32)), -to_c(c_im.astype(F32))], axis=1).astype(BF16)
    a_re = jnp.real(a_bar).reshape(2, N_STATE)
    a_im = jnp.imag(a_bar).reshape(2, N_STATE)
    a_vec = jnp.stack([a_re[0], a_im[0], a_re[1], a_im[1]])
    return bmat, cmat, a_vec


def _sincos_2d(n_tokens):
    rows = n_tokens // GRID_W
    quarter = D_MODEL // 4
    omega = 1.0 / (10000.0 ** (jnp.arange(quarter, dtype=F32) / quarter))
    ang_r = jnp.arange(rows, dtype=F32)[:, None] * omega
    ang_c = jnp.arange(GRID_W, dtype=F32)[:, None] * omega
    emb_r = jnp.concatenate([jnp.sin(ang_r), jnp.cos(ang_r)], -1)
    emb_c = jnp.concatenate([jnp.sin(ang_c), jnp.cos(ang_c)], -1)
    half = D_MODEL // 2
    pos = jnp.concatenate([jnp.broadcast_to(emb_r[:, None], (rows, GRID_W, half)),
                           jnp.broadcast_to(emb_c[None], (rows, GRID_W, half))], -1)
    return pos.reshape(rows * GRID_W, D_MODEL)


def _pack_state(st_re, st_im):
    nb = st_re.shape[0]
    re = jnp.transpose(st_re.reshape(nb, 2, N_STATE), (1, 0, 2))
    im = jnp.transpose(st_im.reshape(nb, 2, N_STATE), (1, 0, 2))
    return jnp.concatenate([re, im], axis=2).astype(F32)


def _unpack_state(fin):
    nb = fin.shape[1]
    re = jnp.transpose(fin[:, :, :N_STATE], (1, 0, 2)).reshape(nb, 2, SSM_GROUPS, SSM_STATE)
    im = jnp.transpose(fin[:, :, N_STATE:], (1, 0, 2)).reshape(nb, 2, SSM_GROUPS, SSM_STATE)
    return re, im


def _tiles(nb, n):
    return min(n, TILE_ROWS), min(n, SCAN_ROWS // nb)


def kernel(x_prompt, x_sample, c, state_s5_re, state_s5_im, c_ctx, w_ada, b_ada, ffn_w_in, ffn_w_out,
           w_mix_in, w_mix_out, mix_norm_g, ssm_lam_re, ssm_lam_im, ssm_log_dt, ssm_b_re, ssm_b_im,
           ssm_c_re, ssm_c_im, ssm_d, ssm_glu_w, ssm_glu_b, fnet_w, pool_w, pool_scale, sgu_w, sgu_b,
           ln_g, ln_b):
    nb_p, n_p, _ = x_prompt.shape
    nb_s, n_s, _ = x_sample.shape

    cond = jnp.concatenate([c.astype(F32), c_ctx.astype(F32)[None],
                            jnp.zeros((16 - nb_s - 1, D_MODEL), F32)], axis=0)
    mod_all = _modulation(cond, w_ada.astype(F32), b_ada.astype(F32))
    mod_all = mod_all.reshape(DEPTH, 16, N_MOD, D_MODEL)

    fcs = _channel_dft()
    ones_blk = jnp.asarray(np.kron(np.eye(SGU_HEADS), np.full((SGU_CH, SGU_CH), 1.0 / SGU_CH)), BF16)
    radix4 = {n: n >= RADIX4_MIN_LEN for n in (n_p, n_s)}
    tabs = {n: _dft_tables(n // 4 if radix4[n] else n) for n in sorted({n_p, n_s})}
    pos = _sincos_2d(n_s).astype(F32)

    groups = [
        dict(x=x_prompt.astype(F32), nb=nb_p, n=n_p, zero_state=True),
        dict(x=x_sample.astype(F32), nb=nb_s, n=n_s, zero_state=False),
    ]
    new_re, new_im = [], []
    w_in = ffn_w_in.astype(BF16)
    w_out = ffn_w_out.astype(BF16)
    w_mix = w_mix_in.astype(BF16)
    w_mix_o = w_mix_out.astype(BF16)
    lng = ln_g.astype(F32)
    lnb = ln_b.astype(F32)
    for i in range(DEPTH):
        bmat, cmat, a_vec = _s5_params(ssm_lam_re[i], ssm_lam_im[i], ssm_log_dt[i], ssm_b_re[i],
                                       ssm_b_im[i], ssm_c_re[i], ssm_c_im[i])
        sgu_cat = jnp.transpose(sgu_w[i].astype(F32), (1, 0, 2)).reshape(SGU_CHUNK, SGU_HEADS * SGU_CHUNK)
        sgu_bias = jnp.repeat(jnp.transpose(sgu_b[i].astype(F32)), SGU_CH, axis=1)
        small = [
            ssm_d[i].astype(F32).reshape(1, GROUP_W), ssm_glu_w[i].astype(BF16),
            ssm_glu_b[i].astype(F32).reshape(1, GROUP_W),
            _block_diag(fnet_w[i].astype(F32)).astype(BF16), _block_diag(pool_w[i].astype(F32)).astype(BF16),
            pool_scale[i].astype(F32).reshape(1, GROUP_W), ones_blk, sgu_cat.astype(BF16), sgu_bias,
            mix_norm_g[i].astype(F32).reshape(1, N_MIXERS * GROUP_W),
        ]
        consts = [(a, _const_spec(a.shape)) for a in small] + [
            (w_mix_o, _stacked_spec(w_mix_o.shape, i)),
            (w_in, _stacked_spec(w_in.shape, i, 1)), (w_out, _stacked_spec(w_out.shape, i, 1)),
            (lng, _stacked_spec(lng.shape, i)), (lnb, _stacked_spec(lnb.shape, i)),
        ]
        for grp in groups:
            nb, n = grp["nb"], grp["n"]
            tt, ts = _tiles(nb, n)
            if grp["zero_state"]:
                mod = jnp.broadcast_to(mod_all[i, nb_s][None], (nb, N_MOD, D_MODEL))
                h0 = jnp.zeros((2, nb, 2 * N_STATE), F32)
            else:
                mod = mod_all[i, :nb]
                h0 = _pack_state(state_s5_re[:, i], state_s5_im[:, i])
            use_pos = pos if (i == 0 and not grp["zero_state"]) else None
            x1, za, gre, gim, zc, zu, zv = _ffn_mixin(grp["x"], use_pos, mod, w_in, w_out, lng, lnb,
                                                      w_mix, fcs, i, tt)
            yf, yb, fin = _s5_scan(za, bmat, cmat, a_vec, h0, nb, n, ts)
            ctab, stab = tabs[n]
            fr = (_seq_dft_r4 if radix4[n] else _seq_dft)(ctab, stab, gre, gim)
            grp["x"] = _mix_ffn(x1, mod, za, yf, yb, fr, zc, zu, zv, consts, tt)
            if grp["zero_state"]:
                fre, fim = _unpack_state(fin)
                new_re.append(fre)
                new_im.append(fim)
    return (groups[0]["x"], groups[1]["x"], jnp.stack(new_re, axis=1), jnp.stack(new_im, axis=1))
```

```python
import functools
import math

import numpy as np
import jax
import jax.numpy as jnp
from jax import lax
from jax.experimental import pallas as pl
from jax.experimental.pallas import tpu as pltpu

F32 = jnp.float32
BF16 = jnp.bfloat16

D_MODEL = 1024
DEPTH = 4
GRID_W = 64
GROUP_W = 256
N_MIXERS = 4
SSM_CH = 16
SSM_GROUPS = 16
SSM_STATE = 64
N_STATE = SSM_GROUPS * SSM_STATE
FNET_HEADS = 4
FNET_CH = 64
POOL_CH = 64
SGU_HEADS = 4
SGU_CH = 64
SGU_CHUNK = 128
D_FF = 2816
IN_COLS = 5 * GROUP_W
N_MOD = 9
ALPHA = (2 * DEPTH) ** 0.25
LN_EPS = 1e-5

LANES = 128
MXU_DIM = 256
FF_CHUNKS = (6 * MXU_DIM, 5 * MXU_DIM)
TILE_ROWS = 512
SUB_ROWS = 512
SCAN_ROWS = 1024
MOD_ROWS = 16
MOD_COLS = 3072
POOL_HALO = 8
DFT_RADIX = 64
RADIX4_MIN_LEN = 1024
VMEM_LIMIT = 56 * 1024 * 1024


def _dot(a, b):
    return jnp.dot(a, b, preferred_element_type=F32)


def _sigmoid(x):
    return 1.0 / (1.0 + jnp.exp(-x))


def _gelu(x):
    k = math.sqrt(2.0 / math.pi)
    h = 0.5 * x
    return h + h * jnp.tanh(x * (k + (k * 0.044715) * (x * x)))


def _layer_norm(r, g, b):
    mu = jnp.mean(r, axis=-1, keepdims=True)
    d = r - mu
    var = jnp.mean(d * d, axis=-1, keepdims=True)
    return d * lax.rsqrt(var + LN_EPS) * g + b


def _swiglu(h, w_in_ref, w_out_ref):
    acc = None
    lo = 0
    for width in FF_CHUNKS:
        a = _dot(h, w_in_ref[:, lo:lo + width])
        g = _dot(h, w_in_ref[:, D_FF + lo:D_FF + lo + width])
        act = (g * _sigmoid(g) * a).astype(BF16)
        o = _dot(act, w_out_ref[lo:lo + width, :])
        acc = o if acc is None else acc + o
        lo += width
    return acc


def _ffn_sublayer(x, shift, scale, gate, w_in_ref, w_out_ref, g, b):
    h = (x * (1.0 + scale) + shift).astype(BF16)
    f = _swiglu(h, w_in_ref, w_out_ref)
    return _layer_norm(ALPHA * x + 0.5 * gate * f, g, b)


def _mod_kernel(c_ref, w_ref, b_ref, o_ref):
    c = c_ref[...]
    s = c * _sigmoid(c)
    rows = s.shape[0]
    s_hi = s.astype(BF16)
    s_lo = (s - s_hi.astype(F32)).astype(BF16)
    w = w_ref[0]
    w_hi = w.astype(BF16)
    w_lo = (w - w_hi.astype(F32)).astype(BF16)
    top = _dot(jnp.concatenate([s_hi, s_lo], axis=0), w_hi)
    o_ref[0] = top[:rows] + top[rows:] + _dot(s_hi, w_lo) + b_ref[0]


def _modulation(cond, w_ada, b_ada):
    rows = cond.shape[0]
    tn = MOD_COLS
    n_col = N_MOD * D_MODEL
    return pl.pallas_call(
        _mod_kernel,
        grid=(DEPTH, n_col // tn),
        in_specs=[
            pl.BlockSpec((rows, D_MODEL), lambda l, j: (0, 0)),
            pl.BlockSpec((1, D_MODEL, tn), lambda l, j: (l, 0, j)),
            pl.BlockSpec((1, 1, tn), lambda l, j: (l, 0, j)),
        ],
        out_specs=pl.BlockSpec((1, rows, tn), lambda l, j: (l, 0, j)),
        out_shape=jax.ShapeDtypeStruct((DEPTH, rows, n_col), F32),
        compiler_params=pltpu.CompilerParams(
            dimension_semantics=("parallel", "parallel"), vmem_limit_bytes=VMEM_LIMIT),
        name="adaln_modulation",
    )(cond, w_ada, b_ada.reshape(DEPTH, 1, n_col))


def _dft_table_kernel(ca_ref, sa_ref, cb_ref, sb_ref, c_ref, s_ref):
    ca = ca_ref[0]
    sa = sa_ref[0]
    cb = cb_ref[...]
    sb = sb_ref[...]
    c_ref[...] = (ca * cb - sa * sb).astype(BF16)
    s_ref[...] = (sa * cb + ca * sb).astype(BF16)


def _dft_tables(n):
    r = DFT_RADIX
    k = np.arange(n, dtype=np.int64)
    ang_a = 2.0 * np.pi * ((r * np.arange(n // r, dtype=np.int64)[:, None] * k[None]) % n) / n
    ang_b = 2.0 * np.pi * ((np.arange(r, dtype=np.int64)[:, None] * k[None]) % n) / n
    ca = jnp.asarray(np.cos(ang_a), F32).reshape(n // r, 1, n)
    sa = jnp.asarray(np.sin(ang_a), F32).reshape(n // r, 1, n)
    cb = jnp.asarray(np.cos(ang_b), F32)
    sb = jnp.asarray(np.sin(ang_b), F32)
    return pl.pallas_call(
        _dft_table_kernel,
        grid=(n // r,),
        in_specs=[
            pl.BlockSpec((1, 1, n), lambda i: (i, 0, 0)),
            pl.BlockSpec((1, 1, n), lambda i: (i, 0, 0)),
            pl.BlockSpec((r, n), lambda i: (0, 0)),
            pl.BlockSpec((r, n), lambda i: (0, 0)),
        ],
        out_specs=[pl.BlockSpec((r, n), lambda i: (i, 0)), pl.BlockSpec((r, n), lambda i: (i, 0))],
        out_shape=[jax.ShapeDtypeStruct((n, n), BF16), jax.ShapeDtypeStruct((n, n), BF16)],
        compiler_params=pltpu.CompilerParams(
            dimension_semantics=("parallel",), vmem_limit_bytes=VMEM_LIMIT),
        name="dft_tables",
    )(ca, sa, cb, sb)


def _ffn_mixin_kernel(has_pos, tt, *refs):
    if has_pos:
        x_ref, pos_ref = refs[0], refs[1]
        refs = refs[2:]
    else:
        x_ref, pos_ref = refs[0], None
        refs = refs[1:]
    (mod_ref, w_in_ref, w_out_ref, lng_ref, lnb_ref, wmix_ref, fcs_ref,
     x1_ref, za_ref, gre_ref, gim_ref, zc_ref, zu_ref, zv_ref) = refs
    m = lambda k: mod_ref[0, k:k + 1, :]
    sub = min(tt, SUB_ROWS)
    for r0 in range(0, tt, sub):
        rows = slice(r0, r0 + sub)
        x = x_ref[0, rows, :]
        if pos_ref is not None:
            x = x + pos_ref[rows, :]
        x1 = _ffn_sublayer(x, m(0), m(1), m(2), w_in_ref, w_out_ref, lng_ref[0:1, :], lnb_ref[0:1, :])
        x1_ref[0, rows, :] = x1
        hm = (x1 * (1.0 + m(4)) + m(3)).astype(BF16)
        z = _dot(hm, wmix_ref[...])
        za_ref[rows, :] = z[:, 0:GROUP_W]
        g = _dot(z[:, GROUP_W:2 * GROUP_W].astype(BF16), fcs_ref[...])
        gre_ref[rows, :] = g[:, 0:GROUP_W].astype(BF16)
        gim_ref[rows, :] = g[:, GROUP_W:2 * GROUP_W].astype(BF16)
        zc_ref[rows, :] = z[:, 2 * GROUP_W:3 * GROUP_W]
        zu_ref[rows, :] = z[:, 3 * GROUP_W:4 * GROUP_W]
        zv_ref[rows, :] = z[:, 4 * GROUP_W:5 * GROUP_W]


def _const_spec(shape):
    nd = len(shape)
    return pl.BlockSpec(shape, lambda b, i: (0,) * nd, pipeline_mode=pl.Buffered(1))


def _stacked_spec(shape, *lead):
    nd = len(shape) - len(lead)
    return pl.BlockSpec((None,) * len(lead) + tuple(shape[len(lead):]),
                        lambda b, i: tuple(lead) + (0,) * nd, pipeline_mode=pl.Buffered(1))


def _ffn_mixin(x, pos, mod, w_in, w_out, ln_g, ln_b, w_mix, fcs, layer, tt):
    nb, n, _ = x.shape
    has_pos = pos is not None
    tm_spec = pl.BlockSpec((tt, GROUP_W), lambda b, i: (i, b))
    in_specs = [pl.BlockSpec((1, tt, D_MODEL), lambda b, i: (b, i, 0))]
    args = [x]
    if has_pos:
        in_specs.append(pl.BlockSpec((tt, D_MODEL), lambda b, i: (i, 0)))
        args.append(pos)
    in_specs += [
        pl.BlockSpec((1, N_MOD, D_MODEL), lambda b, i: (b, 0, 0)),
        _stacked_spec(w_in.shape, layer, 0), _stacked_spec(w_out.shape, layer, 0),
        _stacked_spec(ln_g.shape, layer), _stacked_spec(ln_b.shape, layer),
        _stacked_spec(w_mix.shape, layer), _const_spec(fcs.shape),
    ]
    args += [mod, w_in, w_out, ln_g, ln_b, w_mix, fcs]
    tm_shape = lambda dt: jax.ShapeDtypeStruct((n, nb * GROUP_W), dt)
    return pl.pallas_call(
        functools.partial(_ffn_mixin_kernel, has_pos, tt),
        grid=(nb, n // tt),
        in_specs=in_specs,
        out_specs=[pl.BlockSpec((1, tt, D_MODEL), lambda b, i: (b, i, 0))] + [tm_spec] * 6,
        out_shape=[jax.ShapeDtypeStruct(x.shape, F32), tm_shape(F32), tm_shape(BF16), tm_shape(BF16),
                   tm_shape(F32), tm_shape(F32), tm_shape(F32)],
        compiler_params=pltpu.CompilerParams(
            dimension_semantics=("parallel", "parallel"), vmem_limit_bytes=VMEM_LIMIT),
        name="ffn_mixin",
    )(*args)


def _interleave(z_ref, slab_ref, nb, tt):
    for b in range(nb):
        for j in range(GROUP_W // LANES):
            lo = b * GROUP_W + j * LANES
            slab_ref[j, pl.ds(b, tt, stride=nb), :] = z_ref[:, lo:lo + LANES]
    return jnp.concatenate([slab_ref[j] for j in range(GROUP_W // LANES)], axis=1)


def _deinterleave(y, slab_ref, o_ref, nb, tt):
    for j in range(GROUP_W // LANES):
        slab_ref[j] = y[:, j * LANES:(j + 1) * LANES]
    for b in range(nb):
        for j in range(GROUP_W // LANES):
            lo = b * GROUP_W + j * LANES
            o_ref[:, lo:lo + LANES] = slab_ref[j, pl.ds(b, tt, stride=nb), :]


def _s5_scan_kernel(nb, tt, zf_ref, zb_ref, bmat_ref, cmat_ref, a_ref, h0_ref,
                    yf_ref, yb_ref, fin_ref, hsf_ref, hsb_ref, st_ref, slab_ref):
    i = pl.program_id(0)

    @pl.when(i == 0)
    def _():
        st_ref[...] = h0_ref[...]

    cw = MXU_DIM
    for d, (z_ref, hs_ref, y_ref) in enumerate(((zf_ref, hsf_ref, yf_ref), (zb_ref, hsb_ref, yb_ref))):
        u = _interleave(z_ref, slab_ref.at[d], nb, tt).astype(BF16)
        y = None
        for c in range(N_STATE // cw):
            re = slice(c * cw, (c + 1) * cw)
            im = slice(N_STATE + c * cw, N_STATE + (c + 1) * cw)
            hs_ref[:, re] = _dot(u, bmat_ref[d, :, re])
            hs_ref[:, im] = _dot(u, bmat_ref[d, :, im])
            ar = jnp.broadcast_to(a_ref[2 * d:2 * d + 1, re], (nb, cw))
            ai = jnp.broadcast_to(a_ref[2 * d + 1:2 * d + 2, re], (nb, cw))
            hr = st_ref[d, :, re]
            hi = st_ref[d, :, im]
            for t in range(tt):
                r0 = (t if d == 0 else tt - 1 - t) * nb
                hr, hi = (ar * hr - ai * hi + hs_ref[r0:r0 + nb, re],
                          ar * hi + ai * hr + hs_ref[r0:r0 + nb, im])
                hs_ref[r0:r0 + nb, re] = hr
                hs_ref[r0:r0 + nb, im] = hi
            st_ref[d, :, re] = hr
            st_ref[d, :, im] = hi
            part = (_dot(hs_ref[:, re].astype(BF16), cmat_ref[d, re, :])
                    + _dot(hs_ref[:, im].astype(BF16), cmat_ref[d, im, :]))
            y = part if y is None else y + part
        _deinterleave(y, slab_ref.at[2 + d], y_ref, nb, tt)
    fin_ref[...] = st_ref[...]


def _s5_scan(za, bmat, cmat, a_vec, h0, nb, n, tt):
    rows = tt * nb
    nt = n // tt
    cs = lambda shape: pl.BlockSpec(shape, lambda i: (0,) * len(shape))
    return pl.pallas_call(
        functools.partial(_s5_scan_kernel, nb, tt),
        grid=(nt,),
        in_specs=[
            pl.BlockSpec((tt, nb * GROUP_W), lambda i: (i, 0)),
            pl.BlockSpec((tt, nb * GROUP_W), lambda i: (nt - 1 - i, 0)),
            cs(bmat.shape), cs(cmat.shape), cs(a_vec.shape), cs(h0.shape),
        ],
        out_specs=[
            pl.BlockSpec((tt, nb * GROUP_W), lambda i: (i, 0)),
            pl.BlockSpec((tt, nb * GROUP_W), lambda i: (nt - 1 - i, 0)),
            cs(h0.shape),
        ],
        out_shape=[jax.ShapeDtypeStruct(za.shape, F32), jax.ShapeDtypeStruct(za.shape, F32),
                   jax.ShapeDtypeStruct(h0.shape, F32)],
        scratch_shapes=[pltpu.VMEM((rows, 2 * N_STATE), F32), pltpu.VMEM((rows, 2 * N_STATE), F32),
                        pltpu.VMEM(h0.shape, F32),
                        pltpu.VMEM((4, GROUP_W // LANES, rows, LANES), F32)],
        compiler_params=pltpu.CompilerParams(
            dimension_semantics=("arbitrary",), vmem_limit_bytes=VMEM_LIMIT),
        name="s5_scan",
    )(za, za, bmat, cmat, a_vec, h0)


def _seq_dft_kernel(c_ref, s_ref, gre_ref, gim_ref, o_ref):
    o_ref[...] = _dot(c_ref[...], gre_ref[...]) + _dot(s_ref[...], gim_ref[...])


def _seq_dft(ctab, stab, gre, gim):
    n, cols = gre.shape
    tl = min(n, MXU_DIM)
    resident =pl.BlockSpec((n, cols), lambda i: (0, 0), pipeline_mode=pl.Buffered(1))
    return pl.pallas_call(
        _seq_dft_kernel,
        grid=(n // tl,),
        in_specs=[
            pl.BlockSpec((tl, n), lambda i: (i, 0)),
            pl.BlockSpec((tl, n), lambda i: (i, 0)),
            resident, resident,
        ],
        out_specs=pl.BlockSpec((tl, cols), lambda i: (i, 0)),
        out_shape=jax.ShapeDtypeStruct((n, cols), F32),
        compiler_params=pltpu.CompilerParams(
            dimension_semantics=("parallel",), vmem_limit_bytes=VMEM_LIMIT),
        name="seq_dft",
    )(ctab, stab, gre, gim)


def _seq_dft_r4_kernel(n, gre_ref, gim_ref, c_ref, s_ref, twc_ref, tws_ref, o_ref, slab_ref):
    q = n // 4
    xr = [gre_ref[k * q:(k + 1) * q, :].astype(F32) for k in range(4)]
    xi = [gim_ref[k * q:(k + 1) * q, :].astype(F32) for k in range(4)]
    ar, ai = xr[0] + xr[2], xi[0] + xi[2]
    br, bi = xr[1] + xr[3], xi[1] + xi[3]
    cr, ci = xr[0] - xr[2], xi[0] - xi[2]
    dr, di = xr[1] - xr[3], xi[1] - xi[3]
    u = ((ar + br, ai + bi), (cr + di, ci - dr), (ar - br, ai - bi), (cr - di, ci + dr))
    for r, (ur, ui) in enumerate(u):
        if r == 0:
            yr, yi = ur, ui
        else:
            cs = twc_ref[r - 1]
            sn = tws_ref[r - 1]
            yr = ur * cs + ui * sn
            yi = ui * cs - ur * sn
        x = _dot(c_ref[...], yr.astype(BF16)) + _dot(s_ref[...], yi.astype(BF16))
        for j in range(GROUP_W // LANES):
            slab_ref[j, pl.ds(r, q, stride=4), :] = x[:, j * LANES:(j + 1) * LANES]
    for j in range(GROUP_W // LANES):
        o_ref[:, j * LANES:(j + 1) * LANES] = slab_ref[j]


def _seq_dft_r4(ctab, stab, gre, gim):
    n, cols = gre.shape
    q = n // 4
    ang = 2.0 * np.pi * (np.arange(q)[None, :, None] * np.arange(1, 4)[:, None, None]) / n
    twc = jnp.broadcast_to(jnp.asarray(np.cos(ang), F32), (3, q, GROUP_W))
    tws = jnp.broadcast_to(jnp.asarray(np.sin(ang), F32), (3, q, GROUP_W))
    res = lambda shape: pl.BlockSpec(shape, lambda j: (0,) * len(shape), pipeline_mode=pl.Buffered(1))
    col_spec = pl.BlockSpec((n, GROUP_W), lambda j: (0, j))
    return pl.pallas_call(
        functools.partial(_seq_dft_r4_kernel, n),
        grid=(cols // GROUP_W,),
        in_specs=[col_spec, col_spec, res((q, q)), res((q, q)), res(twc.shape), res(tws.shape)],
        out_specs=col_spec,
        out_shape=jax.ShapeDtypeStruct((n, cols), F32),
        scratch_shapes=[pltpu.VMEM((GROUP_W // LANES, n, LANES), F32)],
        compiler_params=pltpu.CompilerParams(
            dimension_semantics=("parallel",), vmem_limit_bytes=VMEM_LIMIT),
        name="seq_dft_r4",
    )(gre, gim, ctab, stab, twc, tws)


def _seg_mean(x, ones_ref):
    hi = x.astype(BF16)
    lo = (x - hi.astype(F32)).astype(BF16)
    return _dot(hi, ones_ref[...]) + _dot(lo, ones_ref[...])


def _group_norm(y, g):
    return y * lax.rsqrt(jnp.mean(y * y, axis=-1, keepdims=True) + LN_EPS) * g


def _mix_ffn_kernel(n, tt, dft_scale,
                    x_ref, mod_ref, za_ref, yf_ref, yb_ref, fr_ref, zc_ref, zcp_ref, zcn_ref, zu_ref, zv_ref,
                    dskip_ref, gluw_ref, glub_ref, fnw_ref, plw_ref, pls_ref, ones_ref, sguw_ref, sgub_ref,
                    mng_ref, wout_ref, w_in_ref, w_out_ref, lng_ref, lnb_ref, o_ref):
    i = pl.program_id(1)
    last = pl.num_programs(1) - 1
    m = lambda k: mod_ref[0, k:k + 1, :]
    lane_grp = lax.broadcasted_iota(jnp.int32, (1, GROUP_W), 1) // POOL_CH
    half = jnp.left_shift(1, lane_grp)
    sub = min(tt, SUB_ROWS)
    prows = sub + 2 * POOL_HALO
    ahead = lambda v, k: pltpu.roll(v, prows - k, axis=0)

    for r0 in range(0, tt, sub):
        rows = slice(r0, r0 + sub)

        ya = za_ref[rows, :] * dskip_ref[...] + yf_ref[rows, :] + yb_ref[rows, :]
        ya = _gelu(ya)
        ya = ya * _sigmoid(_dot(ya.astype(BF16), gluw_ref[...]) + glub_ref[...])

        yb = _dot((fr_ref[rows, :] * dft_scale).astype(BF16), fnw_ref[...])

        zc = zc_ref[rows, :]
        if r0 == 0:
            prev = jnp.where(i > 0, zcp_ref[...], 0.0)
        else:
            prev = zc_ref[r0 - POOL_HALO:r0, :]
        if r0 + sub == tt:
            nxt = jnp.where(i < last, zcn_ref[...], 0.0)
        else:
            nxt = zc_ref[r0 + sub:r0 + sub + POOL_HALO, :]
        padded = jnp.concatenate([prev, zc, nxt], axis=0)
        s2 = padded + ahead(padded, 1)
        s4 = s2 + ahead(s2, 2)
        s8 = s4 + ahead(s4, 4)
        s16 = s8 + ahead(s8, 8)
        win = jnp.where(lane_grp == 0, ahead(s2, 7)[:sub],
                        jnp.where(lane_grp == 1, ahead(s4, 6)[:sub],
                                  jnp.where(lane_grp == 2, ahead(s8, 4)[:sub], s16[:sub])))
        t_glob = i * tt + r0 + lax.broadcasted_iota(jnp.int32, (sub, GROUP_W), 0)
        cnt = jnp.minimum(t_glob + half, n) - jnp.maximum(t_glob - half, 0)
        pooled = win / cnt.astype(F32)
        yc = _dot((pooled - zc).astype(BF16), plw_ref[...]) * pls_ref[...]

        u = _gelu(zu_ref[rows, :])
        v = _gelu(zv_ref[rows, :])
        dv = v - _seg_mean(v, ones_ref)
        vn = dv * lax.rsqrt(_seg_mean(dv * dv, ones_ref) + LN_EPS)
        parts = []
        for c in range(sub // SGU_CHUNK):
            vc = vn[c * SGU_CHUNK:(c + 1) * SGU_CHUNK]
            stacked = jnp.concatenate(
                [jnp.where(lane_grp == g, vc, 0.0).astype(BF16) for g in range(SGU_HEADS)], axis=0)
            parts.append(_dot(sguw_ref[...], stacked) + sgub_ref[...])
        yd = u * jnp.concatenate(parts, axis=0)

        yn = jnp.concatenate(
            [_group_norm(y, mng_ref[:, k * GROUP_W:(k + 1) * GROUP_W]).astype(BF16)
             for k, y in enumerate((ya, yb, yc, yd))], axis=1)
        acc = _dot(yn, wout_ref[...])
        x = x_ref[0, rows, :]
        x2 = _layer_norm(ALPHA * x + m(5) * acc, lng_ref[1:2, :], lnb_ref[1:2, :])

        o_ref[0, rows, :] = _ffn_sublayer(x2, m(6), m(7), m(8), w_in_ref, w_out_ref,
                                          lng_ref[2:3, :], lnb_ref[2:3, :])


def _mix_ffn(x, mod, za, yf, yb, fr, zc, zu, zv, consts, tt):
    nb, n, _ = x.shape
    tm_spec = pl.BlockSpec((tt, GROUP_W), lambda b, i: (i, b))
    hb = tt // POOL_HALO
    n_hb = n // POOL_HALO
    prev_spec = pl.BlockSpec((POOL_HALO, GROUP_W), lambda b, i: (jnp.maximum(i * hb - 1, 0), b))
    next_spec = pl.BlockSpec((POOL_HALO, GROUP_W), lambda b, i: (jnp.minimum((i + 1) * hb, n_hb - 1), b))
    x_spec = pl.BlockSpec((1, tt, D_MODEL), lambda b, i: (b, i, 0))
    in_specs = [x_spec, pl.BlockSpec((1, N_MOD, D_MODEL), lambda b, i: (b, 0, 0)),
                tm_spec, tm_spec, tm_spec, tm_spec, tm_spec, prev_spec, next_spec, tm_spec, tm_spec]
    in_specs += [spec for _, spec in consts]
    consts = [arr for arr, _ in consts]
    dft_scale = 1.0 / math.sqrt(n * FNET_CH)
    return pl.pallas_call(
        functools.partial(_mix_ffn_kernel, n, tt, dft_scale),
        grid=(nb, n // tt),
        in_specs=in_specs,
        out_specs=x_spec,
        out_shape=jax.ShapeDtypeStruct(x.shape, F32),
        compiler_params=pltpu.CompilerParams(
            dimension_semantics=("parallel", "parallel"), vmem_limit_bytes=VMEM_LIMIT),
        name="mix_ffn",
    )(x, mod, za, yf, yb, fr, zc, zc, zc, zu, zv, *consts)


def _block_diag(w):
    h, c, d = w.shape
    eye = jnp.eye(h, dtype=w.dtype)
    return jnp.einsum("gcd,gk->gckd", w, eye).reshape(h * c, h * d)


def _channel_dft():
    k = np.arange(FNET_CH)
    ang = 2.0 * np.pi * ((k[:, None] * k[None]) % FNET_CH) / FNET_CH
    eye = np.eye(FNET_HEADS)
    fc = np.kron(eye, np.cos(ang))
    fs = np.kron(eye, np.sin(ang))
    return jnp.asarray(np.concatenate([fc, -fs], axis=1), BF16)


def _s5_params(lam_re, lam_im, log_dt, b_re, b_im, c_re, c_im):
    lam = lax.complex(lam_re.astype(F32), lam_im.astype(F32))
    dt = jnp.exp(log_dt.astype(F32))[..., None]
    a_bar = jnp.exp(lam * dt)
    b_bar = ((a_bar - 1.0) / lam)[..., None] * lax.complex(b_re.astype(F32), b_im.astype(F32))
    eye = jnp.eye(SSM_GROUPS, dtype=F32)
    to_b = lambda w: jnp.einsum("dgph,gk->dghkp", w, eye).reshape(2, GROUP_W, N_STATE)
    bmat = jnp.concatenate([to_b(jnp.real(b_bar)), to_b(jnp.imag(b_bar))], axis=2).astype(BF16)
    to_c = lambda w: jnp.einsum("dghp,gk->dgpkh", w, eye).reshape(2, N_STATE, GROUP_W)
    cmat = jnp.concatenate([to_c(c_re.astype(F32)), -to_c(c_im.astype(F32))], axis=1).astype(BF16)
    a_re = jnp.real(a_bar).reshape(2, N_STATE)
    a_im = jnp.imag(a_bar).reshape(2, N_STATE)
    a_vec = jnp.stack([a_re[0], a_im[0], a_re[1], a_im[1]])
    return bmat, cmat, a_vec


def _sincos_2d(n_tokens):
    rows = n_tokens // GRID_W
    quarter = D_MODEL // 4
    omega = 1.0 / (10000.0 ** (jnp.arange(quarter, dtype=F32) / quarter))
    ang_r = jnp.arange(rows, dtype=F32)[:, None] * omega
    ang_c = jnp.arange(GRID_W, dtype=F32)[:, None] * omega
    emb_r = jnp.concatenate([jnp.sin(ang_r), jnp.cos(ang_r)], -1)
    emb_c = jnp.concatenate([jnp.sin(ang_c), jnp.cos(ang_c)], -1)
    half = D_MODEL // 2
    pos = jnp.concatenate([jnp.broadcast_to(emb_r[:, None], (rows, GRID_W, half)),
                           jnp.broadcast_to(emb_c[None], (rows, GRID_W, half))], -1)
    return pos.reshape(rows * GRID_W, D_MODEL)


def _pack_state(st_re, st_im):
    nb = st_re.shape[0]
    re = jnp.transpose(st_re.reshape(nb, 2, N_STATE), (1, 0, 2))
    im = jnp.transpose(st_im.reshape(nb, 2, N_STATE), (1, 0, 2))
    return jnp.concatenate([re, im], axis=2).astype(F32)


def _unpack_state(fin):
    nb = fin.shape[1]
    re = jnp.transpose(fin[:, :, :N_STATE], (1, 0, 2)).reshape(nb, 2, SSM_GROUPS, SSM_STATE)
    im = jnp.transpose(fin[:, :, N_STATE:], (1, 0, 2)).reshape(nb, 2, SSM_GROUPS, SSM_STATE)
    return re, im


def _tiles(nb, n):
    return min(n, TILE_ROWS), min(n, SCAN_ROWS // nb)


def kernel(x_prompt, x_sample, c, state_s5_re, state_s5_im, c_ctx, w_ada, b_ada, ffn_w_in, ffn_w_out,
           w_mix_in, w_mix_out, mix_norm_g, ssm_lam_re, ssm_lam_im, ssm_log_dt, ssm_b_re, ssm_b_im,
           ssm_c_re, ssm_c_im, ssm_d, ssm_glu_w, ssm_glu_b, fnet_w, pool_w, pool_scale, sgu_w, sgu_b,
           ln_g, ln_b):
    nb_p, n_p, _ = x_prompt.shape
    nb_s, n_s, _ = x_sample.shape

    cond = jnp.concatenate([c.astype(F32), c_ctx.astype(F32)[None],
                            jnp.zeros((MOD_ROWS - nb_s - 1, D_MODEL), F32)], axis=0)
    mod_all = _modulation(cond, w_ada.astype(F32), b_ada.astype(F32))
    mod_all = mod_all.reshape(DEPTH, MOD_ROWS, N_MOD, D_MODEL)

    fcs = _channel_dft()
    ones_blk = jnp.asarray(np.kron(np.eye(SGU_HEADS), np.full((SGU_CH, SGU_CH), 1.0 / SGU_CH)), BF16)
    radix4 = {n: n >= RADIX4_MIN_LEN for n in (n_p, n_s)}
    tabs = {n: _dft_tables(n // 4 if radix4[n] else n) for n in sorted({n_p, n_s})}
    pos = _sincos_2d(n_s).astype(F32)

    groups = [
        dict(x=x_prompt.astype(F32), nb=nb_p, n=n_p, zero_state=True),
        dict(x=x_sample.astype(F32), nb=nb_s, n=n_s, zero_state=False),
    ]
    new_re, new_im = [], []
    w_in = ffn_w_in.astype(BF16)
    w_out = ffn_w_out.astype(BF16)
    w_mix = w_mix_in.astype(BF16)
    w_mix_o = w_mix_out.astype(BF16)
    lng = ln_g.astype(F32)
    lnb = ln_b.astype(F32)
    for i in range(DEPTH):
        bmat, cmat, a_vec = _s5_params(ssm_lam_re[i], ssm_lam_im[i], ssm_log_dt[i], ssm_b_re[i],
                                       ssm_b_im[i], ssm_c_re[i], ssm_c_im[i])
        sgu_cat = jnp.transpose(sgu_w[i].astype(F32), (1, 0, 2)).reshape(SGU_CHUNK, SGU_HEADS * SGU_CHUNK)
        sgu_bias = jnp.repeat(jnp.transpose(sgu_b[i].astype(F32)), SGU_CH, axis=1)
        small = [
            ssm_d[i].astype(F32).reshape(1, GROUP_W), ssm_glu_w[i].astype(BF16),
            ssm_glu_b[i].astype(F32).reshape(1, GROUP_W),
            _block_diag(fnet_w[i].astype(F32)).astype(BF16), _block_diag(pool_w[i].astype(F32)).astype(BF16),
            pool_scale[i].astype(F32).reshape(1, GROUP_W), ones_blk, sgu_cat.astype(BF16), sgu_bias,
            mix_norm_g[i].astype(F32).reshape(1, N_MIXERS * GROUP_W),
        ]
        consts = [(a, _const_spec(a.shape)) for a in small] + [
            (w_mix_o, _stacked_spec(w_mix_o.shape, i)),
            (w_in, _stacked_spec(w_in.shape, i, 1)), (w_out, _stacked_spec(w_out.shape, i, 1)),
            (lng, _stacked_spec(lng.shape, i)), (lnb, _stacked_spec(lnb.shape, i)),
        ]
        for grp in groups:
            nb, n = grp["nb"], grp["n"]
            tt, ts = _tiles(nb, n)
            if grp["zero_state"]:
                mod = jnp.broadcast_to(mod_all[i, nb_s][None], (nb, N_MOD, D_MODEL))
                h0 = jnp.zeros((2, nb, 2 * N_STATE), F32)
            else:
                mod = mod_all[i, :nb]
                h0 = _pack_state(state_s5_re[:, i], state_s5_im[:, i])
            use_pos = pos if (i == 0 and not grp["zero_state"]) else None
            x1, za, gre, gim, zc, zu, zv = _ffn_mixin(grp["x"], use_pos, mod, w_in, w_out, lng, lnb,
                                                      w_mix, fcs, i, tt)
            yf, yb, fin = _s5_scan(za, bmat, cmat, a_vec, h0, nb, n, ts)
            ctab, stab = tabs[n]
            fr = (_seq_dft_r4 if radix4[n] else _seq_dft)(ctab, stab, gre, gim)
            grp["x"] = _mix_ffn(x1, mod, za, yf, yb, fr, zc, zu, zv, consts, tt)
            if grp["zero_state"]:
                fre, fim = _unpack_state(fin)
                new_re.append(fre)
                new_im.append(fim)
    return (groups[0]["x"], groups[1]["x"], jnp.stack(new_re, axis=1), jnp.stack(new_im, axis=1))
```

```python
import functools
import math

import numpy as np
import jax
import jax.numpy as jnp
from jax import lax
from jax.experimental import pallas as pl
from jax.experimental.pallas import tpu as pltpu

F32 = jnp.float32
BF16 = jnp.bfloat16

D_MODEL = 1024
DEPTH = 4
GRID_W = 64
GROUP_W = 256
N_MIXERS = 4
SSM_CH = 16
SSM_GROUPS = 16
SSM_STATE = 64
N_STATE = SSM_GROUPS * SSM_STATE
FNET_HEADS = 4
FNET_CH = 64
POOL_CH = 64
SGU_HEADS = 4
SGU_CH = 64
SGU_CHUNK = 128
D_FF = 2816
IN_COLS = 5 * GROUP_W
N_MOD = 9
ALPHA = (2 * DEPTH) ** 0.25
LN_EPS = 1e-5

LANES = 128
MXU_DIM = 256
FF_CHUNKS = (6 * MXU_DIM, 5 * MXU_DIM)
TILE_ROWS = 512
SUB_ROWS = 512
SCAN_ROWS = 1024
MOD_ROWS = 16
MOD_COLS = 3072
POOL_HALO = 8
DFT_RADIX = 64
RADIX4_MIN_LEN = 1024
VMEM_LIMIT = 56 * 1024 * 1024


def _dot(a, b):
    return jnp.dot(a, b, preferred_element_type=F32)


def _sigmoid(x):
    return 1.0 / (1.0 + jnp.exp(-x))


def _gelu(x):
    k = math.sqrt(2.0 / math.pi)
    h = 0.5 * x
    return h + h * jnp.tanh(x * (k + (k * 0.044715) * (x * x)))


def _layer_norm(r, g, b):
    mu = jnp.mean(r, axis=-1, keepdims=True)
    d = r - mu
    var = jnp.mean(d * d, axis=-1, keepdims=True)
    return d * lax.rsqrt(var + LN_EPS) * g + b


def _swiglu(h, w_in_ref, w_out_ref):
    acc = None
    lo = 0
    for width in FF_CHUNKS:
        a = _dot(h, w_in_ref[:, lo:lo + width])
        g = _dot(h, w_in_ref[:, D_FF + lo:D_FF + lo + width])
        act = (g * _sigmoid(g) * a).astype(BF16)
        o = _dot(act, w_out_ref[lo:lo + width, :])
        acc = o if acc is None else acc + o
        lo += width
    return acc


def _ffn_sublayer(x, shift, scale, gate, w_in_ref, w_out_ref, g, b):
    h = (x * (1.0 + scale) + shift).astype(BF16)
    f = _swiglu(h, w_in_ref, w_out_ref)
    return _layer_norm(ALPHA * x + 0.5 * gate * f, g, b)


def _mod_kernel(c_ref, w_ref, b_ref, o_ref):
    c = c_ref[...]
    s = c * _sigmoid(c)
    rows = s.shape[0]
    s_hi = s.astype(BF16)
    s_lo = (s - s_hi.astype(F32)).astype(BF16)
    w = w_ref[0]
    w_hi = w.astype(BF16)
    w_lo = (w - w_hi.astype(F32)).astype(BF16)
    top = _dot(jnp.concatenate([s_hi, s_lo], axis=0), w_hi)
    o_ref[0] = top[:rows] + top[rows:] + _dot(s_hi, w_lo) + b_ref[0]


def _modulation(cond, w_ada, b_ada):
    rows = cond.shape[0]
    tn = MOD_COLS
    n_col = N_MOD * D_MODEL
    return pl.pallas_call(
        _mod_kernel,
        grid=(DEPTH, n_col // tn),
        in_specs=[
            pl.BlockSpec((rows, D_MODEL), lambda l, j: (0, 0)),
            pl.BlockSpec((1, D_MODEL, tn), lambda l, j: (l, 0, j)),
            pl.BlockSpec((1, 1, tn), lambda l, j: (l, 0, j)),
        ],
        out_specs=pl.BlockSpec((1, rows, tn), lambda l, j: (l, 0, j)),
        out_shape=jax.ShapeDtypeStruct((DEPTH, rows, n_col), F32),
        compiler_params=pltpu.CompilerParams(
            dimension_semantics=("parallel", "parallel"), vmem_limit_bytes=VMEM_LIMIT),
        name="adaln_modulation",
    )(cond, w_ada, b_ada.reshape(DEPTH, 1, n_col))


def _dft_table_kernel(ca_ref, sa_ref, cb_ref, sb_ref, c_ref, s_ref):
    ca = ca_ref[0]
    sa = sa_ref[0]
    cb = cb_ref[...]
    sb = sb_ref[...]
    c_ref[...] = (ca * cb - sa * sb).astype(BF16)
    s_ref[...] = (sa * cb + ca * sb).astype(BF16)


def _dft_tables(n):
    r = DFT_RADIX
    k = np.arange(n, dtype=np.int64)
    ang_a = 2.0 * np.pi * ((r * np.arange(n // r, dtype=np.int64)[:, None] * k[None]) % n) / n
    ang_b = 2.0 * np.pi * ((np.arange(r, dtype=np.int64)[:, None] * k[None]) % n) / n
    ca = jnp.asarray(np.cos(ang_a), F32).reshape(n // r, 1, n)
    sa = jnp.asarray(np.sin(ang_a), F32).reshape(n // r, 1, n)
    cb = jnp.asarray(np.cos(ang_b), F32)
    sb = jnp.asarray(np.sin(ang_b), F32)
    return pl.pallas_call(
        _dft_table_kernel,
        grid=(n // r,),
        in_specs=[
            pl.BlockSpec((1, 1, n), lambda i: (i, 0, 0)),
            pl.BlockSpec((1, 1, n), lambda i: (i, 0, 0)),
            pl.BlockSpec((r, n), lambda i: (0, 0)),
            pl.BlockSpec((r, n), lambda i: (0, 0)),
        ],
        out_specs=[pl.BlockSpec((r, n), lambda i: (i, 0)), pl.BlockSpec((r, n), lambda i: (i, 0))],
        out_shape=[jax.ShapeDtypeStruct((n, n), BF16), jax.ShapeDtypeStruct((n, n), BF16)],
        compiler_params=pltpu.CompilerParams(
            dimension_semantics=("parallel",), vmem_limit_bytes=VMEM_LIMIT),
        name="dft_tables",
    )(ca, sa, cb, sb)


def _ffn_mixin_kernel(has_pos, tt, *refs):
    if has_pos:
        x_ref, pos_ref = refs[0], refs[1]
        refs = refs[2:]
    else:
        x_ref, pos_ref = refs[0], None
        refs = refs[1:]
    (mod_ref, w_in_ref, w_out_ref, lng_ref, lnb_ref, wmix_ref, fcs_ref,
     x1_ref, za_ref, gre_ref, gim_ref, zc_ref, zu_ref, zv_ref) = refs
    m = lambda k: mod_ref[0, k:k + 1, :]
    sub = min(tt, SUB_ROWS)
    for r0 in range(0, tt, sub):
        rows = slice(r0, r0 + sub)
        x = x_ref[0, rows, :]
        if pos_ref is not None:
            x = x + pos_ref[rows, :]
        x1 = _ffn_sublayer(x, m(0), m(1), m(2), w_in_ref, w_out_ref, lng_ref[0:1, :], lnb_ref[0:1, :])
        x1_ref[0, rows, :] = x1
        hm = (x1 * (1.0 + m(4)) + m(3)).astype(BF16)
        z = _dot(hm, wmix_ref[...])
        za_ref[rows, :] = z[:, 0:GROUP_W]
        g = _dot(z[:, GROUP_W:2 * GROUP_W].astype(BF16), fcs_ref[...])
        gre_ref[rows, :] = g[:, 0:GROUP_W].astype(BF16)
        gim_ref[rows, :] = g[:, GROUP_W:2 * GROUP_W].astype(BF16)
        zc_ref[rows, :] = z[:, 2 * GROUP_W:3 * GROUP_W]
        zu_ref[rows, :] = z[:, 3 * GROUP_W:4 * GROUP_W]
        zv_ref[rows, :] = z[:, 4 * GROUP_W:5 * GROUP_W]


def _const_spec(shape):
    nd = len(shape)
    return pl.BlockSpec(shape, lambda b, i: (0,) * nd, pipeline_mode=pl.Buffered(1))


def _stacked_spec(shape, *lead):
    nd = len(shape) - len(lead)
    return pl.BlockSpec((None,) * len(lead) + tuple(shape[len(lead):]),
                        lambda b, i: tuple(lead) + (0,) * nd, pipeline_mode=pl.Buffered(1))


def _ffn_mixin(x, pos, mod, w_in, w_out, ln_g, ln_b, w_mix, fcs, layer, tt):
    nb, n, _ = x.shape
    has_pos = pos is not None
    tm_spec = pl.BlockSpec((tt, GROUP_W), lambda b, i: (i, b))
    in_specs = [pl.BlockSpec((1, tt, D_MODEL), lambda b, i: (b, i, 0))]
    args = [x]
    if has_pos:
        in_specs.append(pl.BlockSpec((tt, D_MODEL), lambda b, i: (i, 0)))
        args.append(pos)
    in_specs += [
        pl.BlockSpec((1, N_MOD, D_MODEL), lambda b, i: (b, 0, 0)),
        _stacked_spec(w_in.shape, layer, 0), _stacked_spec(w_out.shape, layer, 0),
        _stacked_spec(ln_g.shape, layer), _stacked_spec(ln_b.shape, layer),
        _stacked_spec(w_mix.shape, layer), _const_spec(fcs.shape),
    ]
    args += [mod, w_in, w_out, ln_g, ln_b, w_mix, fcs]
    tm_shape = lambda dt: jax.ShapeDtypeStruct((n, nb * GROUP_W), dt)
    out_specs = [pl.BlockSpec((1, tt, D_MODEL), lambda b, i: (b, i, 0))] + [tm_spec] * 6
    n_stream = 3 if has_pos else 2

    def outer(*refs):
        stream_refs, param_refs = refs[:n_stream], refs[n_stream:n_stream + 6]
        out_refs = refs[n_stream + 6:]

        def body(*blocks):
            _ffn_mixin_kernel(has_pos, tt, *blocks[:n_stream], *param_refs, *blocks[n_stream:])

        pltpu.emit_pipeline(body, grid=(nb, n // tt), in_specs=in_specs[:n_stream],
                            out_specs=out_specs)(*stream_refs, *out_refs)

    whole = lambda shape, *lead: pl.BlockSpec(
        (None,) * len(lead) + tuple(shape[len(lead):]),
        lambda _: tuple(lead) + (0,) * (len(shape) - len(lead)), pipeline_mode=pl.Buffered(1))
    any_spec = pl.BlockSpec(memory_space=pl.ANY)
    return pl.pallas_call(
        outer,
        grid=(1,),
        in_specs=[any_spec] * n_stream + [
            whole(w_in.shape, layer, 0), whole(w_out.shape, layer, 0), whole(ln_g.shape, layer),
            whole(ln_b.shape, layer), whole(w_mix.shape, layer), whole(fcs.shape)],
        out_specs=[any_spec] * 7,
        out_shape=[jax.ShapeDtypeStruct(x.shape, F32), tm_shape(F32), tm_shape(BF16), tm_shape(BF16),
                   tm_shape(F32), tm_shape(F32), tm_shape(F32)],
        compiler_params=pltpu.CompilerParams(
            dimension_semantics=("arbitrary",), vmem_limit_bytes=VMEM_LIMIT),
        name="ffn_mixin",
    )(*args)


def _interleave(z_ref, slab_ref, nb, tt):
    for b in range(nb):
        for j in range(GROUP_W // LANES):
            lo = b * GROUP_W + j * LANES
            slab_ref[j, pl.ds(b, tt, stride=nb), :] = z_ref[:, lo:lo + LANES]
    return jnp.concatenate([slab_ref[j] for j in range(GROUP_W // LANES)], axis=1)


def _deinterleave(y, slab_ref, o_ref, nb, tt):
    for j in range(GROUP_W // LANES):
        slab_ref[j] = y[:, j * LANES:(j + 1) * LANES]
    for b in range(nb):
        for j in range(GROUP_W // LANES):
            lo = b * GROUP_W + j * LANES
            o_ref[:, lo:lo + LANES] = slab_ref[j, pl.ds(b, tt, stride=nb), :]


def _s5_scan_kernel(nb, tt, zf_ref, zb_ref, bmat_ref, cmat_ref, a_ref, h0_ref,
                    yf_ref, yb_ref, fin_ref, hsf_ref, hsb_ref, st_ref, slab_ref):
    i = pl.program_id(0)

    @pl.when(i == 0)
    def _():
        st_ref[...] = h0_ref[...]

    cw = MXU_DIM
    for d, (z_ref, hs_ref, y_ref) in enumerate(((zf_ref, hsf_ref, yf_ref), (zb_ref, hsb_ref, yb_ref))):
        u = _interleave(z_ref, slab_ref.at[d], nb, tt).astype(BF16)
        y = None
        for c in range(N_STATE // cw):
            re = slice(c * cw, (c + 1) * cw)
            im = slice(N_STATE + c * cw, N_STATE + (c + 1) * cw)
            hs_ref[:, re] = _dot(u, bmat_ref[d, :, re])
            hs_ref[:, im] = _dot(u, bmat_ref[d, :, im])
            ar = jnp.broadcast_to(a_ref[2 * d:2 * d + 1, re], (nb, cw))
            ai = jnp.broadcast_to(a_ref[2 * d + 1:2 * d + 2, re], (nb, cw))
            hr = st_ref[d, :, re]
            hi = st_ref[d, :, im]
            for t in range(tt):
                r0 = (t if d == 0 else tt - 1 - t) * nb
                hr, hi = (ar * hr - ai * hi + hs_ref[r0:r0 + nb, re],
                          ar * hi + ai * hr + hs_ref[r0:r0 + nb, im])
                hs_ref[r0:r0 + nb, re] = hr
                hs_ref[r0:r0 + nb, im] = hi
            st_ref[d, :, re] = hr
            st_ref[d, :, im] = hi
            part = (_dot(hs_ref[:, re].astype(BF16), cmat_ref[d, re, :])
                    + _dot(hs_ref[:, im].astype(BF16), cmat_ref[d, im, :]))
            y = part if y is None else y + part
        _deinterleave(y, slab_ref.at[2 + d], y_ref, nb, tt)
    fin_ref[...] = st_ref[...]


def _s5_scan(za, bmat, cmat, a_vec, h0, nb, n, tt):
    rows = tt * nb
    nt = n // tt
    cs = lambda shape: pl.BlockSpec(shape, lambda i: (0,) * len(shape))
    return pl.pallas_call(
        functools.partial(_s5_scan_kernel, nb, tt),
        grid=(nt,),
        in_specs=[
            pl.BlockSpec((tt, nb * GROUP_W), lambda i: (i, 0)),
            pl.BlockSpec((tt, nb * GROUP_W), lambda i: (nt - 1 - i, 0)),
            cs(bmat.shape), cs(cmat.shape), cs(a_vec.shape), cs(h0.shape),
        ],
        out_specs=[
            pl.BlockSpec((tt, nb * GROUP_W), lambda i: (i, 0)),
            pl.BlockSpec((tt, nb * GROUP_W), lambda i: (nt - 1 - i, 0)),
            cs(h0.shape),
        ],
        out_shape=[jax.ShapeDtypeStruct(za.shape, F32), jax.ShapeDtypeStruct(za.shape, F32),
                   jax.ShapeDtypeStruct(h0.shape, F32)],
        scratch_shapes=[pltpu.VMEM((rows, 2 * N_STATE), F32), pltpu.VMEM((rows, 2 * N_STATE), F32),
                        pltpu.VMEM(h0.shape, F32),
                        pltpu.VMEM((4, GROUP_W // LANES, rows, LANES), F32)],
        compiler_params=pltpu.CompilerParams(
            dimension_semantics=("arbitrary",), vmem_limit_bytes=VMEM_LIMIT),
        name="s5_scan",
    )(za, za, bmat, cmat, a_vec, h0)


def _seq_dft_kernel(c_ref, s_ref, gre_ref, gim_ref, o_ref):
    o_ref[...] = _dot(c_ref[...], gre_ref[...]) + _dot(s_ref[...], gim_ref[...])


def _seq_dft(ctab, stab, gre, gim):
    n, cols = gre.shape
    tl = min(n, MXU_DIM)
    resident =pl.BlockSpec((n, cols), lambda i: (0, 0), pipeline_mode=pl.Buffered(1))
    return pl.pallas_call(
        _seq_dft_kernel,
        grid=(n // tl,),
        in_specs=[
            pl.BlockSpec((tl, n), lambda i: (i, 0)),
            pl.BlockSpec((tl, n), lambda i: (i, 0)),
            resident, resident,
        ],
        out_specs=pl.BlockSpec((tl, cols), lambda i: (i, 0)),
        out_shape=jax.ShapeDtypeStruct((n, cols), F32),
        compiler_params=pltpu.CompilerParams(
            dimension_semantics=("parallel",), vmem_limit_bytes=VMEM_LIMIT),
        name="seq_dft",
    )(ctab, stab, gre, gim)


def _seq_dft_r4_kernel(n, gre_ref, gim_ref, c_ref, s_ref, twc_ref, tws_ref, o_ref, slab_ref):
    q = n // 4
    xr = [gre_ref[k * q:(k + 1) * q, :].astype(F32) for k in range(4)]
    xi = [gim_ref[k * q:(k + 1) * q, :].astype(F32) for k in range(4)]
    ar, ai = xr[0] + xr[2], xi[0] + xi[2]
    br, bi = xr[1] + xr[3], xi[1] + xi[3]
    cr, ci = xr[0] - xr[2], xi[0] - xi[2]
    dr, di = xr[1] - xr[3], xi[1] - xi[3]
    u = ((ar + br, ai + bi), (cr + di, ci - dr), (ar - br, ai - bi), (cr - di, ci + dr))
    for r, (ur, ui) in enumerate(u):
        if r == 0:
            yr, yi = ur, ui
        else:
            cs = twc_ref[r - 1]
            sn = tws_ref[r - 1]
            yr = ur * cs + ui * sn
            yi = ui * cs - ur * sn
        x = _dot(c_ref[...], yr.astype(BF16)) + _dot(s_ref[...], yi.astype(BF16))
        for j in range(GROUP_W // LANES):
            slab_ref[j, pl.ds(r, q, stride=4), :] = x[:, j * LANES:(j + 1) * LANES]
    for j in range(GROUP_W // LANES):
        o_ref[:, j * LANES:(j + 1) * LANES] = slab_ref[j]


def _seq_dft_r4(ctab, stab, gre, gim):
    n, cols = gre.shape
    q = n // 4
    ang = 2.0 * np.pi * (np.arange(q)[None, :, None] * np.arange(1, 4)[:, None, None]) / n
    twc = jnp.broadcast_to(jnp.asarray(np.cos(ang), F32), (3, q, GROUP_W))
    tws = jnp.broadcast_to(jnp.asarray(np.sin(ang), F32), (3, q, GROUP_W))
    res = lambda shape: pl.BlockSpec(shape, lambda j: (0,) * len(shape), pipeline_mode=pl.Buffered(1))
    col_spec = pl.BlockSpec((n, GROUP_W), lambda j: (0, j))
    return pl.pallas_call(
        functools.partial(_seq_dft_r4_kernel, n),
        grid=(cols // GROUP_W,),
        in_specs=[col_spec, col_spec, res((q, q)), res((q, q)), res(twc.shape), res(tws.shape)],
        out_specs=col_spec,
        out_shape=jax.ShapeDtypeStruct((n, cols), F32),
        scratch_shapes=[pltpu.VMEM((GROUP_W // LANES, n, LANES), F32)],
        compiler_params=pltpu.CompilerParams(
            dimension_semantics=("parallel",), vmem_limit_bytes=VMEM_LIMIT),
        name="seq_dft_r4",
    )(gre, gim, ctab, stab, twc, tws)


def _seg_mean(x, ones_ref):
    hi = x.astype(BF16)
    lo = (x - hi.astype(F32)).astype(BF16)
    return _dot(hi, ones_ref[...]) + _dot(lo, ones_ref[...])


def _group_norm(y, g):
    return y * lax.rsqrt(jnp.mean(y * y, axis=-1, keepdims=True) + LN_EPS) * g


def _mix_ffn_kernel(n, tt, dft_scale,
                    x_ref, mod_ref, za_ref, yf_ref, yb_ref, fr_ref, zc_ref, zcp_ref, zcn_ref, zu_ref, zv_ref,
                    dskip_ref, gluw_ref, glub_ref, fnw_ref, plw_ref, pls_ref, ones_ref, sguw_ref, sgub_ref,
                    mng_ref, wout_ref, w_in_ref, w_out_ref, lng_ref, lnb_ref, o_ref):
    i = pl.program_id(1)
    last = pl.num_programs(1) - 1
    m = lambda k: mod_ref[0, k:k + 1, :]
    lane_grp = lax.broadcasted_iota(jnp.int32, (1, GROUP_W), 1) // POOL_CH
    half = jnp.left_shift(1, lane_grp)
    sub = min(tt, SUB_ROWS)
    prows = sub + 2 * POOL_HALO
    ahead = lambda v, k: pltpu.roll(v, prows - k, axis=0)

    for r0 in range(0, tt, sub):
        rows = slice(r0, r0 + sub)

        ya = za_ref[rows, :] * dskip_ref[...] + yf_ref[rows, :] + yb_ref[rows, :]
        ya = _gelu(ya)
        ya = ya * _sigmoid(_dot(ya.astype(BF16), gluw_ref[...]) + glub_ref[...])

        yb = _dot((fr_ref[rows, :] * dft_scale).astype(BF16), fnw_ref[...])

        zc = zc_ref[rows, :]
        if r0 == 0:
            prev = jnp.where(i > 0, zcp_ref[...], 0.0)
        else:
            prev = zc_ref[r0 - POOL_HALO:r0, :]
        if r0 + sub == tt:
            nxt = jnp.where(i < last, zcn_ref[...], 0.0)
        else:
            nxt = zc_ref[r0 + sub:r0 + sub + POOL_HALO, :]
        padded = jnp.concatenate([prev, zc, nxt], axis=0)
        s2 = padded + ahead(padded, 1)
        s4 = s2 + ahead(s2, 2)
        s8 = s4 + ahead(s4, 4)
        s16 = s8 + ahead(s8, 8)
        win = jnp.where(lane_grp == 0, ahead(s2, 7)[:sub],
                        jnp.where(lane_grp == 1, ahead(s4, 6)[:sub],
                                  jnp.where(lane_grp == 2, ahead(s8, 4)[:sub], s16[:sub])))
        t_glob = i * tt + r0 + lax.broadcasted_iota(jnp.int32, (sub, GROUP_W), 0)
        cnt = jnp.minimum(t_glob + half, n) - jnp.maximum(t_glob - half, 0)
        pooled = win / cnt.astype(F32)
        yc = _dot((pooled - zc).astype(BF16), plw_ref[...]) * pls_ref[...]

        u = _gelu(zu_ref[rows, :])
        v = _gelu(zv_ref[rows, :])
        dv = v - _seg_mean(v, ones_ref)
        vn = dv * lax.rsqrt(_seg_mean(dv * dv, ones_ref) + LN_EPS)
        parts = []
        for c in range(sub // SGU_CHUNK):
            vc = vn[c * SGU_CHUNK:(c + 1) * SGU_CHUNK]
            stacked = jnp.concatenate(
                [jnp.where(lane_grp == g, vc, 0.0).astype(BF16) for g in range(SGU_HEADS)], axis=0)
            parts.append(_dot(sguw_ref[...], stacked) + sgub_ref[...])
        yd = u * jnp.concatenate(parts, axis=0)

        yn = jnp.concatenate(
            [_group_norm(y, mng_ref[:, k * GROUP_W:(k + 1) * GROUP_W]).astype(BF16)
             for k, y in enumerate((ya, yb, yc, yd))], axis=1)
        acc = _dot(yn, wout_ref[...])
        x = x_ref[0, rows, :]
        x2 = _layer_norm(ALPHA * x + m(5) * acc, lng_ref[1:2, :], lnb_ref[1:2, :])

        o_ref[0, rows, :] = _ffn_sublayer(x2, m(6), m(7), m(8), w_in_ref, w_out_ref,
                                          lng_ref[2:3, :], lnb_ref[2:3, :])


def _mix_ffn(x, mod, za, yf, yb, fr, zc, zu, zv, consts, tt):
    nb, n, _ = x.shape
    tm_spec = pl.BlockSpec((tt, GROUP_W), lambda b, i: (i, b))
    hb = tt // POOL_HALO
    n_hb = n // POOL_HALO
    prev_spec = pl.BlockSpec((POOL_HALO, GROUP_W), lambda b, i: (jnp.maximum(i * hb - 1, 0), b))
    next_spec = pl.BlockSpec((POOL_HALO, GROUP_W), lambda b, i: (jnp.minimum((i + 1) * hb, n_hb - 1), b))
    x_spec = pl.BlockSpec((1, tt, D_MODEL), lambda b, i: (b, i, 0))
    in_specs = [x_spec, pl.BlockSpec((1, N_MOD, D_MODEL), lambda b, i: (b, 0, 0)),
                tm_spec, tm_spec, tm_spec, tm_spec, tm_spec, prev_spec, next_spec, tm_spec, tm_spec]
    in_specs += [spec for _, spec in consts]
    consts = [arr for arr, _ in consts]
    dft_scale = 1.0 / math.sqrt(n * FNET_CH)
    return pl.pallas_call(
        functools.partial(_mix_ffn_kernel, n, tt, dft_scale),
        grid=(nb, n // tt),
        in_specs=in_specs,
        out_specs=x_spec,
        out_shape=jax.ShapeDtypeStruct(x.shape, F32),
        compiler_params=pltpu.CompilerParams(
            dimension_semantics=("parallel", "parallel"), vmem_limit_bytes=VMEM_LIMIT),
        name="mix_ffn",
    )(x, mod, za, yf, yb, fr, zc, zc, zc, zu, zv, *consts)


def _block_diag(w):
    h, c, d = w.shape
    eye = jnp.eye(h, dtype=w.dtype)
    return jnp.einsum("gcd,gk->gckd", w, eye).reshape(h * c, h * d)


def _channel_dft():
    k = np.arange(FNET_CH)
    ang = 2.0 * np.pi * ((k[:, None] * k[None]) % FNET_CH) / FNET_CH
    eye = np.eye(FNET_HEADS)
    fc = np.kron(eye, np.cos(ang))
    fs = np.kron(eye, np.sin(ang))
    return jnp.asarray(np.concatenate([fc, -fs], axis=1), BF16)


def _s5_params(lam_re, lam_im, log_dt, b_re, b_im, c_re, c_im):
    lam = lax.complex(lam_re.astype(F32), lam_im.astype(F32))
    dt = jnp.exp(log_dt.astype(F32))[..., None]
    a_bar = jnp.exp(lam * dt)
    b_bar = ((a_bar - 1.0) / lam)[..., None] * lax.complex(b_re.astype(F32), b_im.astype(F32))
    eye = jnp.eye(SSM_GROUPS, dtype=F32)
    to_b = lambda w: jnp.einsum("dgph,gk->dghkp", w, eye).reshape(2, GROUP_W, N_STATE)
    bmat = jnp.concatenate([to_b(jnp.real(b_bar)), to_b(jnp.imag(b_bar))], axis=2).astype(BF16)
    to_c = lambda w: jnp.einsum("dghp,gk->dgpkh", w, eye).reshape(2, N_STATE, GROUP_W)
    cmat = jnp.concatenate([to_c(c_re.astype(F32)), -to_c(c_im.astype(F32))], axis=1).astype(BF16)
    a_re = jnp.real(a_bar).reshape(2, N_STATE)
    a_im = jnp.imag(a_bar).reshape(2, N_STATE)
    a_vec = jnp.stack([a_re[0], a_im[0], a_re[1], a_im[1]])
    return bmat, cmat, a_vec


def _sincos_2d(n_tokens):
    rows = n_tokens // GRID_W
    quarter = D_MODEL // 4
    omega = 1.0 / (10000.0 ** (jnp.arange(quarter, dtype=F32) / quarter))
    ang_r = jnp.arange(rows, dtype=F32)[:, None] * omega
    ang_c = jnp.arange(GRID_W, dtype=F32)[:, None] * omega
    emb_r = jnp.concatenate([jnp.sin(ang_r), jnp.cos(ang_r)], -1)
    emb_c = jnp.concatenate([jnp.sin(ang_c), jnp.cos(ang_c)], -1)
    half = D_MODEL // 2
    pos = jnp.concatenate([jnp.broadcast_to(emb_r[:, None], (rows, GRID_W, half)),
                           jnp.broadcast_to(emb_c[None], (rows, GRID_W, half))], -1)
    return pos.reshape(rows * GRID_W, D_MODEL)


def _pack_state(st_re, st_im):
    nb = st_re.shape[0]
    re = jnp.transpose(st_re.reshape(nb, 2, N_STATE), (1, 0, 2))
    im = jnp.transpose(st_im.reshape(nb, 2, N_STATE), (1, 0, 2))
    return jnp.concatenate([re, im], axis=2).astype(F32)


def _unpack_state(fin):
    nb = fin.shape[1]
    re = jnp.transpose(fin[:, :, :N_STATE], (1, 0, 2)).reshape(nb, 2, SSM_GROUPS, SSM_STATE)
    im = jnp.transpose(fin[:, :, N_STATE:], (1, 0, 2)).reshape(nb, 2, SSM_GROUPS, SSM_STATE)
    return re, im


def _tiles(nb, n):
    return min(n, TILE_ROWS), min(n, SCAN_ROWS // nb)


def kernel(x_prompt, x_sample, c, state_s5_re, state_s5_im, c_ctx, w_ada, b_ada, ffn_w_in, ffn_w_out,
           w_mix_in, w_mix_out, mix_norm_g, ssm_lam_re, ssm_lam_im, ssm_log_dt, ssm_b_re, ssm_b_im,
           ssm_c_re, ssm_c_im, ssm_d, ssm_glu_w, ssm_glu_b, fnet_w, pool_w, pool_scale, sgu_w, sgu_b,
           ln_g, ln_b):
    nb_p, n_p, _ = x_prompt.shape
    nb_s, n_s, _ = x_sample.shape

    cond = jnp.concatenate([c.astype(F32), c_ctx.astype(F32)[None],
                            jnp.zeros((MOD_ROWS - nb_s - 1, D_MODEL), F32)], axis=0)
    mod_all = _modulation(cond, w_ada.astype(F32), b_ada.astype(F32))
    mod_all = mod_all.reshape(DEPTH, MOD_ROWS, N_MOD, D_MODEL)

    fcs = _channel_dft()
    ones_blk = jnp.asarray(np.kron(np.eye(SGU_HEADS), np.full((SGU_CH, SGU_CH), 1.0 / SGU_CH)), BF16)
    radix4 = {n: n >= RADIX4_MIN_LEN for n in (n_p, n_s)}
    tabs = {n: _dft_tables(n // 4 if radix4[n] else n) for n in sorted({n_p, n_s})}
    pos = _sincos_2d(n_s).astype(F32)

    groups = [
        dict(x=x_prompt.astype(F32), nb=nb_p, n=n_p, zero_state=True),
        dict(x=x_sample.astype(F32), nb=nb_s, n=n_s, zero_state=False),
    ]
    new_re, new_im = [], []
    w_in = ffn_w_in.astype(BF16)
    w_out = ffn_w_out.astype(BF16)
    w_mix = w_mix_in.astype(BF16)
    w_mix_o = w_mix_out.astype(BF16)
    lng = ln_g.astype(F32)
    lnb = ln_b.astype(F32)
    for i in range(DEPTH):
        bmat, cmat, a_vec = _s5_params(ssm_lam_re[i], ssm_lam_im[i], ssm_log_dt[i], ssm_b_re[i],
                                       ssm_b_im[i], ssm_c_re[i], ssm_c_im[i])
        sgu_cat = jnp.transpose(sgu_w[i].astype(F32), (1, 0, 2)).reshape(SGU_CHUNK, SGU_HEADS * SGU_CHUNK)
        sgu_bias = jnp.repeat(jnp.transpose(sgu_b[i].astype(F32)), SGU_CH, axis=1)
        small = [
            ssm_d[i].astype(F32).reshape(1, GROUP_W), ssm_glu_w[i].astype(BF16),
            ssm_glu_b[i].astype(F32).reshape(1, GROUP_W),
            _block_diag(fnet_w[i].astype(F32)).astype(BF16), _block_diag(pool_w[i].astype(F32)).astype(BF16),
            pool_scale[i].astype(F32).reshape(1, GROUP_W), ones_blk, sgu_cat.astype(BF16), sgu_bias,
            mix_norm_g[i].astype(F32).reshape(1, N_MIXERS * GROUP_W),
        ]
        consts = [(a, _const_spec(a.shape)) for a in small] + [
            (w_mix_o, _stacked_spec(w_mix_o.shape, i)),
            (w_in, _stacked_spec(w_in.shape, i, 1)), (w_out, _stacked_spec(w_out.shape, i, 1)),
            (lng, _stacked_spec(lng.shape, i)), (lnb, _stacked_spec(lnb.shape, i)),
        ]
        for grp in groups:
            nb, n = grp["nb"], grp["n"]
            tt, ts = _tiles(nb, n)
            if grp["zero_state"]:
                mod = jnp.broadcast_to(mod_all[i, nb_s][None], (nb, N_MOD, D_MODEL))
                h0 = jnp.zeros((2, nb, 2 * N_STATE), F32)
            else:
                mod = mod_all[i, :nb]
                h0 = _pack_state(state_s5_re[:, i], state_s5_im[:, i])
            use_pos = pos if (i == 0 and not grp["zero_state"]) else None
            x1, za, gre, gim, zc, zu, zv = _ffn_mixin(grp["x"], use_pos, mod, w_in, w_out, lng, lnb,
                                                      w_mix, fcs, i, tt)
            yf, yb, fin = _s5_scan(za, bmat, cmat, a_vec, h0, nb, n, ts)
            ctab, stab = tabs[n]
            fr = (_seq_dft_r4 if radix4[n] else _seq_dft)(ctab, stab, gre, gim)
            grp["x"] = _mix_ffn(x1, mod, za, yf, yb, fr, zc, zu, zv, consts, tt)
            if grp["zero_state"]:
                fre, fim = _unpack_state(fin)
                new_re.append(fre)
                new_im.append(fim)
    return (groups[0]["x"], groups[1]["x"], jnp.stack(new_re, axis=1), jnp.stack(new_im, axis=1))
```
